```python
import jax, jax.numpy as jnp
from jax import lax
import numpy as np

D_MODEL = 2048
BATCH = 16
SEQ = 256
DEPTH = 2
DEC_BATCH = 2
DEC_SEQ = 1024
PAST_LEN = 256

GRID_W = 64
A_W = 1024
A_KERNEL = 3
N_HEADS = 16
Q_RANK = 768
KV_RANK = 512
NOPE_DIM = 128
ROPE_DIM = 64
V_DIM = 128
QK_DIM = NOPE_DIM + ROPE_DIM
C_W = 1024
C_KERNEL = 31
D_FF = 5632
F_KERNEL = 3
Q_BLOCK = 128
ROPE_THETA = 10000.0
EPS = 1e-6
SPLITS = (A_W, A_W, A_W, Q_RANK, KV_RANK + ROPE_DIM, 2 * C_W, D_MODEL, D_MODEL, D_MODEL)
IN_W = sum(SPLITS)

kernel_name = "hybrid_diffusion_prefix_trunk_step"


def rmsnorm(x, g):
    xf = x.astype(jnp.float32)
    y = xf * lax.rsqrt(jnp.mean(xf * xf, axis=-1, keepdims=True) + EPS)
    return y.astype(x.dtype) * g


def layernorm(x, g, b):
    xf = x.astype(jnp.float32)
    mu = jnp.mean(xf, axis=-1, keepdims=True)
    var = jnp.mean(jnp.square(xf - mu), axis=-1, keepdims=True)
    return ((xf - mu) * lax.rsqrt(var + EPS)).astype(x.dtype) * g + b


def dwconv(x, w):
    k = w.shape[0]
    return lax.conv_general_dilated(
        x, w[:, None, :].astype(x.dtype), window_strides=(1,), padding=[(k // 2, k // 2)],
        dimension_numbers=("NWC", "WIO", "NWC"), feature_group_count=x.shape[-1])


def axial_rope_tables(n_tokens):
    rows = n_tokens // GRID_W
    r = jnp.repeat(jnp.arange(rows), GRID_W).astype(jnp.float32)
    col = jnp.tile(jnp.arange(GRID_W), rows).astype(jnp.float32)
    half_axis = ROPE_DIM // 2
    inv = ROPE_THETA ** (-jnp.arange(0, half_axis, 2, dtype=jnp.float32) / half_axis)
    ar = r[:, None, None] * inv
    ac = col[:, None, None] * inv
    return (jnp.cos(ar), jnp.sin(ar), jnp.cos(ac), jnp.sin(ac))


def _rot(x, cos, sin):
    h = x.shape[-1] // 2
    x1, x2 = x[..., :h], x[..., h:]
    cos = cos.astype(x.dtype)
    sin = sin.astype(x.dtype)
    return jnp.concatenate([x1 * cos - x2 * sin, x2 * cos + x1 * sin], axis=-1)


def axial_rope(x, rope):
    cos_r, sin_r, cos_c, sin_c = rope
    h = ROPE_DIM // 2
    return jnp.concatenate([_rot(x[..., :h], cos_r, sin_r), _rot(x[..., h:], cos_c, sin_c)], axis=-1)


def attention(q, k, v):
    b, lq, h, dk = q.shape
    nb = lq // Q_BLOCK
    qb = q.reshape(b, nb, Q_BLOCK, h, dk).transpose(1, 0, 2, 3, 4)
    scale = QK_DIM ** -0.5

    def one(qblk):
        s = jnp.einsum("bqhd,bkhd->bhqk", qblk, k, preferred_element_type=jnp.float32) * scale
        pr = jax.nn.softmax(s, axis=-1)
        return jnp.einsum("bhqk,bkhd->bqhd", pr.astype(v.dtype), v)

    o = lax.map(one, qb)
    return o.transpose(1, 0, 2, 3, 4).reshape(b, lq, h, v.shape[-1])


def expand_kv(ckv, w_ukv):
    b, l, _ = ckv.shape
    kv = (ckv @ w_ukv).reshape(b, l, N_HEADS, NOPE_DIM + V_DIM)
    return kv[..., :NOPE_DIM], kv[..., NOPE_DIM:]


def token_mixers(h, p, l, rope, ctx):
    b, L, _ = h.shape
    points = np.cumsum(np.array(SPLITS[:-1])).tolist()
    z = h @ p["w_in"][l]
    a_h, a_b, a_c, q_a, kv_a, glu, gate_a, gate_b, gate_c = jnp.split(z, points, axis=-1)
    y_a = (a_b * dwconv(a_c * a_h, p["conv_a"][l])) @ p["w_a_out"][l]
    q = (rmsnorm(q_a, p["g_q"][l]) @ p["w_uq"][l]).reshape(b, L, N_HEADS, QK_DIM)
    q_nope, q_pe = q[..., :NOPE_DIM], q[..., NOPE_DIM:]
    ckv = rmsnorm(kv_a[..., :KV_RANK], p["g_kv"][l])
    k_pe = kv_a[..., KV_RANK:]
    k_pe_h = k_pe[:, :, None, :]
    if rope is not None:
        q_pe = axial_rope(q_pe, rope)
        k_pe_h = axial_rope(k_pe_h, rope)
    k_nope, v = expand_kv(ckv, p["w_ukv"][l])
    k = jnp.concatenate([k_nope, jnp.broadcast_to(k_pe_h, (b, L, N_HEADS, ROPE_DIM))], axis=-1)
    if ctx is not None:
        ckv_ctx, kpe_ctx = ctx
        lc = ckv_ctx.shape[1]
        kn_c, v_c = expand_kv(ckv_ctx, p["w_ukv"][l])
        k_c = jnp.concatenate([kn_c, jnp.broadcast_to(kpe_ctx[:, :, None, :], (b, lc, N_HEADS, ROPE_DIM))], axis=-1)
        k = jnp.concatenate([k_c, k], axis=1)
        v = jnp.concatenate([v_c, v], axis=1)
    q = jnp.concatenate([q_nope, q_pe], axis=-1)
    y_b = attention(q, k, v).reshape(b, L, N_HEADS * V_DIM) @ p["w_mla_out"][l]
    g1, g2 = jnp.split(glu, 2, axis=-1)
    u = dwconv(g1 * jax.nn.sigmoid(g2), p["conv_c"][l])
    u = layernorm(u, p["ln_c_g"][l], p["ln_c_b"][l])
    y_c = jax.nn.silu(u) @ p["w_c_out"][l]
    merged = (jax.nn.sigmoid(gate_a) * y_a + jax.nn.sigmoid(gate_b) * y_b
              + jax.nn.sigmoid(gate_c) * y_c)
    return merged @ p["w_o"][l], ckv, k_pe


def conv_ffn(h, p, l):
    u = dwconv(h @ p["w_up"][l], p["conv_f"][l])
    a, g = jnp.split(u, 2, axis=-1)
    return (jax.nn.silu(a) * g) @ p["w_down"][l]


def block(x, cvec, p, l, rope, ctx):
    mod = jax.nn.silu(cvec) @ p["w_mod"][l] + p["b_mod"][l]
    sh1, sc1, gt1, sh2, sc2, gt2 = [m[:, None, :] for m in jnp.split(mod, 6, axis=-1)]
    h = rmsnorm(x, p["g_norm1"][l]) * (1 + sc1) + sh1
    mix, ckv, k_pe = token_mixers(h, p, l, rope, ctx)
    x = x + gt1 * mix
    h = rmsnorm(x, p["g_norm2"][l]) * (1 + sc2) + sh2
    x = x + gt2 * conv_ffn(h, p, l)
    return x, ckv, k_pe


def setup_inputs(seed: int = 0) -> dict:
    key = jax.random.key(seed)
    ks = iter(jax.random.split(key, 40))

    def nrm(shape, scale):
        return jax.random.normal(next(ks), shape, jnp.float32) * scale

    def gain(shape):
        return 1.0 + nrm(shape, 0.02)

    Ld = DEPTH
    return {
        "x_prompt": nrm((BATCH, SEQ, D_MODEL), 1.0),
        "x_sample": nrm((DEC_BATCH, DEC_SEQ, D_MODEL), 1.0),
        "cache_ckv": nrm((DEC_BATCH, DEPTH, PAST_LEN, KV_RANK), 1.0),
        "cache_kpe": nrm((DEC_BATCH, DEPTH, PAST_LEN, ROPE_DIM), 1.0),
        "c": nrm((DEC_BATCH, D_MODEL), 1.0),
        "c_ctx": nrm((D_MODEL,), 1.0),
        "w_mod": nrm((Ld, D_MODEL, 6 * D_MODEL), 0.5 * D_MODEL ** -0.5),
        "b_mod": nrm((Ld, 6 * D_MODEL), 0.01),
        "g_norm1": gain((Ld, D_MODEL)),
        "g_norm2": gain((Ld, D_MODEL)),
        "w_in": nrm((Ld, D_MODEL, IN_W), D_MODEL ** -0.5),
        "conv_a": nrm((Ld, A_KERNEL, A_W), A_KERNEL ** -0.5),
        "w_a_out": nrm((Ld, A_W, D_MODEL), A_W ** -0.5),
        "g_q": gain((Ld, Q_RANK)),
        "w_uq": nrm((Ld, Q_RANK, N_HEADS * QK_DIM), Q_RANK ** -0.5),
        "g_kv": gain((Ld, KV_RANK)),
        "w_ukv": nrm((Ld, KV_RANK, N_HEADS * (NOPE_DIM + V_DIM)), KV_RANK ** -0.5),
        "w_mla_out": nrm((Ld, N_HEADS * V_DIM, D_MODEL), (N_HEADS * V_DIM) ** -0.5),
        "conv_c": nrm((Ld, C_KERNEL, C_W), C_KERNEL ** -0.5),
        "ln_c_g": gain((Ld, C_W)),
        "ln_c_b": nrm((Ld, C_W), 0.01),
        "w_c_out": nrm((Ld, C_W, D_MODEL), C_W ** -0.5),
        "w_o": nrm((Ld, D_MODEL, D_MODEL), D_MODEL ** -0.5),
        "w_up": nrm((Ld, D_MODEL, 2 * D_FF), D_MODEL ** -0.5),
        "conv_f": nrm((Ld, F_KERNEL, 2 * D_FF), F_KERNEL ** -0.5),
        "w_down": nrm((Ld, D_FF, D_MODEL), D_FF ** -0.5),
        "g_final": gain((D_MODEL,)),
    }


def reference(x_prompt, x_sample, cache_ckv, cache_kpe, c, c_ctx, w_mod, b_mod, g_norm1, g_norm2,
              w_in, conv_a, w_a_out, g_q, w_uq, g_kv, w_ukv, w_mla_out, conv_c, ln_c_g, ln_c_b,
              w_c_out, w_o, w_up, conv_f, w_down, g_final):
    p = dict(w_mod=w_mod, b_mod=b_mod, g_norm1=g_norm1, g_norm2=g_norm2, w_in=w_in, conv_a=conv_a,
             w_a_out=w_a_out, g_q=g_q, w_uq=w_uq, g_kv=g_kv, w_ukv=w_ukv, w_mla_out=w_mla_out,
             conv_c=conv_c, ln_c_g=ln_c_g, ln_c_b=ln_c_b, w_c_out=w_c_out, w_o=w_o, w_up=w_up,
             conv_f=conv_f, w_down=w_down)
    xp = x_prompt
    ckv_list, kpe_list = [], []
    for l in range(DEPTH):
        xp, ckv, kpe = block(xp, c_ctx[None, :], p, l, None, None)
        ckv_list.append(ckv)
        kpe_list.append(kpe)
    y_prompt = rmsnorm(xp, g_final)
    new_cache_ckv = jnp.stack(ckv_list, axis=1)
    new_cache_kpe = jnp.stack(kpe_list, axis=1)
    rope = axial_rope_tables(x_sample.shape[1])
    xs = x_sample
    for l in range(DEPTH):
        xs, _, _ = block(xs, c, p, l, rope, (cache_ckv[:, l], cache_kpe[:, l]))
    y_sample = rmsnorm(xs, g_final)
    return (y_prompt, y_sample, new_cache_ckv, new_cache_kpe)
```

```python
import functools

import jax
import jax.numpy as jnp
from jax import lax
from jax.experimental import pallas as pl
from jax.experimental.pallas import tpu as pltpu

D_MODEL = 2048
BATCH = 16
SEQ = 256
DEPTH = 2
DEC_BATCH = 2
DEC_SEQ = 1024
PAST_LEN = 256
GRID_W = 64
A_W = 1024
A_KERNEL = 3
N_HEADS = 16
Q_RANK = 768
KV_RANK = 512
NOPE_DIM = 128
ROPE_DIM = 64
V_DIM = 128
QK_DIM = NOPE_DIM + ROPE_DIM
C_W = 1024
C_KERNEL = 31
D_FF = 5632
F_KERNEL = 3
ROPE_THETA = 10000.0
EPS = 1e-6

CTX_TOK = BATCH * SEQ
DEC_TOK = DEC_BATCH * DEC_SEQ
TOK = CTX_TOK + DEC_TOK
TM = 1024
N_ROW_BLOCKS = TOK // TM
CTX_BLOCKS = CTX_TOK // TM
N_COND = 3
COND_ROWS = 8
LANES = 128
SUBLANES = 8
HEAD_PAIRS = N_HEADS // 2
KV_A_W = 640
CONV_PAD = 16
CONV_ROWS = 16
VMEM_LIMIT = 56 * 1024 * 1024

ZQ = 0
ZGATE = 1024
ZA = ZGATE + 3 * D_MODEL
ZGLU = ZA + 3 * A_W
Z_W = ZGLU + 2 * C_W

F32 = jnp.float32
BF16 = jnp.bfloat16


def _params(n_axes):
    return pltpu.CompilerParams(dimension_semantics=("arbitrary",) * n_axes,
                                vmem_limit_bytes=VMEM_LIMIT)


def _cond_row(i):
    return jnp.maximum(i - (CTX_BLOCKS - 1), 0)


def _sigmoid(x):
    return jax.nn.sigmoid(x)


def _silu(x):
    return x * jax.nn.sigmoid(x)


def _seq_pos(shape, i):
    seq_len = jnp.where(i < CTX_BLOCKS, SEQ, DEC_SEQ)
    row = lax.broadcasted_iota(jnp.int32, shape, 0)
    return row & (seq_len - 1), seq_len


def _conv3_rows(u, w, pos, seq_len):
    n = u.shape[0]
    prev = jnp.where(pos == 0, 0.0, pltpu.roll(u, 1, 0))
    nxt = jnp.where(pos == seq_len - 1, 0.0, pltpu.roll(u, n - 1, 0))
    return w[0:1, :] * prev + w[1:2, :] * u + w[2:3, :] * nxt


def _mod_kernel(c_ref, w_ref, b_ref, o_ref):
    s = _silu(c_ref[...]).astype(BF16)
    o_ref[...] = jnp.dot(s, w_ref[...].astype(BF16), preferred_element_type=F32) + b_ref[...]


def _modulation(cond, w_mod, b_mod):
    tn = 1024
    n = 6 * D_MODEL
    return pl.pallas_call(
        _mod_kernel,
        grid=(DEPTH, n // tn),
        in_specs=[pl.BlockSpec((COND_ROWS, D_MODEL), lambda l, j: (0, 0)),
                  pl.BlockSpec((None, D_MODEL, tn), lambda l, j: (l, 0, j)),
                  pl.BlockSpec((None, 1, tn), lambda l, j: (l, 0, j))],
        out_specs=pl.BlockSpec((None, COND_ROWS, tn), lambda l, j: (l, 0, j)),
        out_shape=jax.ShapeDtypeStruct((DEPTH, COND_ROWS, n), F32),
        compiler_params=_params(2),
        name="modulation",
    )(cond, w_mod, b_mod.reshape(DEPTH, 1, n))


def _rms(x):
    return x * lax.rsqrt(jnp.mean(x * x, axis=-1, keepdims=True) + EPS)


def _norm_mod_kernel(x_ref, g_ref, sc_ref, sh_ref, o_ref):
    y = _rms(x_ref[...]) * g_ref[...]
    o_ref[...] = (y * (1.0 + sc_ref[...]) + sh_ref[...]).astype(o_ref.dtype)


def _norm_mod(x, g, sc, sh):
    tr = 512
    per = TM // tr
    cond = lambda i: (_cond_row(i // per), 0, 0)
    return pl.pallas_call(
        _norm_mod_kernel,
        grid=(TOK // tr,),
        in_specs=[pl.BlockSpec((tr, D_MODEL), lambda i: (i, 0)),
                  pl.BlockSpec((1, D_MODEL), lambda i: (0, 0)),
                  pl.BlockSpec((None, 1, D_MODEL), cond),
                  pl.BlockSpec((None, 1, D_MODEL), cond)],
        out_specs=pl.BlockSpec((tr, D_MODEL), lambda i: (i, 0)),
        out_shape=jax.ShapeDtypeStruct((TOK, D_MODEL), BF16),
        compiler_params=_params(1),
        name="norm_mod",
    )(x, g, sc, sh)


def _final_norm_kernel(x_ref, g_ref, o_ref):
    o_ref[...] = _rms(x_ref[...]) * g_ref[...]


def _final_norm(x, g):
    tr = 512
    return pl.pallas_call(
        _final_norm_kernel,
        grid=(TOK // tr,),
        in_specs=[pl.BlockSpec((tr, D_MODEL), lambda i: (i, 0)),
                  pl.BlockSpec((1, D_MODEL), lambda i: (0, 0))],
        out_specs=pl.BlockSpec((tr, D_MODEL), lambda i: (i, 0)),
        out_shape=jax.ShapeDtypeStruct((TOK, D_MODEL), F32),
        compiler_params=_params(1),
        name="final_norm",
    )(x, g)


def _mm_kernel(a_ref, w_ref, o_ref):
    o_ref[...] = jnp.dot(a_ref[...], w_ref[...], preferred_element_type=F32).astype(o_ref.dtype)


def _matmul(a, w, out_dtype, tn, tm=TM, name="matmul"):
    m, k = a.shape
    n = w.shape[1]
    return pl.pallas_call(
        _mm_kernel,
        grid=(m // tm, n // tn),
        in_specs=[pl.BlockSpec((tm, k), lambda i, j: (i, 0)),
                  pl.BlockSpec((k, tn), lambda i, j: (0, j))],
        out_specs=pl.BlockSpec((tm, tn), lambda i, j: (i, j)),
        out_shape=jax.ShapeDtypeStruct((m, n), out_dtype),
        compiler_params=_params(2),
        name=name,
    )(a, w)


def _mm_residual_kernel(a_ref, w_ref, x_ref, gt_ref, o_ref):
    acc = jnp.dot(a_ref[...], w_ref[...], preferred_element_type=F32)
    o_ref[...] = x_ref[...] + gt_ref[...] * acc


def _matmul_residual(a, w, x, gt, tn, name):
    m, k = a.shape
    n = w.shape[1]
    return pl.pallas_call(
        _mm_residual_kernel,
        grid=(m // TM, n // tn),
        in_specs=[pl.BlockSpec((TM, k), lambda i, j: (i, 0)),
                  pl.BlockSpec((k, tn), lambda i, j: (0, j)),
                  pl.BlockSpec((TM, tn), lambda i, j: (i, j)),
                  pl.BlockSpec((None, 1, tn), lambda i, j: (_cond_row(i), 0, j))],
        out_specs=pl.BlockSpec((TM, tn), lambda i, j: (i, j)),
        out_shape=jax.ShapeDtypeStruct((m, n), F32),
        compiler_params=_params(2),
        name=name,
    )(a, w, x, gt)


def _q_kernel(a_ref, g_ref, w_ref, o_ref, an_ref):
    @pl.when(pl.program_id(1) == 0)
    def _():
        an_ref[...] = (_rms(a_ref[...].astype(F32)) * g_ref[...]).astype(BF16)

    o_ref[...] = jnp.dot(an_ref[...], w_ref[...], preferred_element_type=F32).astype(o_ref.dtype)


def _q_proj(z, g_q, w_uq):
    tn = 512
    n = w_uq.shape[1]
    return pl.pallas_call(
        _q_kernel,
        grid=(N_ROW_BLOCKS, n // tn),
        in_specs=[pl.BlockSpec((TM, Q_RANK), lambda i, j: (i, ZQ // Q_RANK)),
                  pl.BlockSpec((1, Q_RANK), lambda i, j: (0, 0)),
                  pl.BlockSpec((Q_RANK, tn), lambda i, j: (0, j))],
        out_specs=pl.BlockSpec((TM, tn), lambda i, j: (i, j)),
        out_shape=jax.ShapeDtypeStruct((TOK, n), BF16),
        scratch_shapes=[pltpu.VMEM((TM, Q_RANK), BF16)],
        compiler_params=_params(2),
        name="q_proj",
    )(z, g_q, w_uq)


def _kv_kernel(a_ref, g_ref, w_ref, ckv_ref, kv_ref, an_ref):
    @pl.when(pl.program_id(1) == 0)
    def _():
        y = _rms(a_ref[:, :KV_RANK]) * g_ref[...]
        ckv_ref[...] = y
        an_ref[...] = y.astype(BF16)

    kv_ref[...] = jnp.dot(an_ref[...], w_ref[...], preferred_element_type=F32).astype(kv_ref.dtype)


def _kv_proj(kv_a, g_kv, w_ukv):
    tn = 1024
    n = w_ukv.shape[1]
    return pl.pallas_call(
        _kv_kernel,
        grid=(N_ROW_BLOCKS, n // tn),
        in_specs=[pl.BlockSpec((TM, KV_A_W), lambda i, j: (i, 0)),
                  pl.BlockSpec((1, KV_RANK), lambda i, j: (0, 0)),
                  pl.BlockSpec((KV_RANK, tn), lambda i, j: (0, j))],
        out_specs=[pl.BlockSpec((TM, KV_RANK), lambda i, j: (i, 0)),
                   pl.BlockSpec((TM, tn), lambda i, j: (i, j))],
        out_shape=[jax.ShapeDtypeStruct((TOK, KV_RANK), F32),
                   jax.ShapeDtypeStruct((TOK, n), BF16)],
        scratch_shapes=[pltpu.VMEM((TM, KV_RANK), BF16)],
        compiler_params=_params(2),
        name="kv_proj",
    )(kv_a, g_kv, w_ukv)


def _merge_kernel(a1_ref, w1_ref, a2_ref, w2_ref, a3_ref, w3_ref, g1_ref, g2_ref, g3_ref, o_ref):
    def branch(a_ref, w_ref, g_ref):
        y = jnp.dot(a_ref[...], w_ref[...], preferred_element_type=F32)
        return _sigmoid(g_ref[...].astype(F32)) * y

    o_ref[...] = (branch(a1_ref, w1_ref, g1_ref) + branch(a2_ref, w2_ref, g2_ref)
                  + branch(a3_ref, w3_ref, g3_ref)).astype(o_ref.dtype)


def _merge(ya, w_a_out, attn, w_mla_out, yc, w_c_out, z):
    tn = 512
    gate = lambda k: (lambda i, j: (i, (ZGATE + k * D_MODEL) // tn + j))
    act = lambda width: pl.BlockSpec((TM, width), lambda i, j: (i, 0))
    wgt = lambda width: pl.BlockSpec((width, tn), lambda i, j: (0, j))
    return pl.pallas_call(
        _merge_kernel,
        grid=(N_ROW_BLOCKS, D_MODEL // tn),
        in_specs=[act(A_W), wgt(A_W), act(N_HEADS * V_DIM), wgt(N_HEADS * V_DIM), act(C_W), wgt(C_W),
                  pl.BlockSpec((TM, tn), gate(0)), pl.BlockSpec((TM, tn), gate(1)),
                  pl.BlockSpec((TM, tn), gate(2))],
        out_specs=pl.BlockSpec((TM, tn), lambda i, j: (i, j)),
        out_shape=jax.ShapeDtypeStruct((TOK, D_MODEL), BF16),
        compiler_params=_params(2),
        name="merge",
    )(ya, w_a_out, attn, w_mla_out, yc, w_c_out, z, z, z)


def _ffn_up_kernel(h_ref, wa_ref, wg_ref, ca_ref, cg_ref, o_ref):
    i = pl.program_id(0)
    h = h_ref[...]
    ua = jnp.dot(h, wa_ref[...], preferred_element_type=F32)
    ug = jnp.dot(h, wg_ref[...], preferred_element_type=F32)
    pos, seq_len = _seq_pos(ua.shape, i)
    a = _conv3_rows(ua, ca_ref[...], pos, seq_len)
    g = _conv3_rows(ug, cg_ref[...], pos, seq_len)
    o_ref[...] = (_silu(a) * g).astype(o_ref.dtype)


def _ffn_up(h, w_up, conv_f):
    tn = 512
    half = D_FF // tn
    return pl.pallas_call(
        _ffn_up_kernel,
        grid=(N_ROW_BLOCKS, half),
        in_specs=[pl.BlockSpec((TM, D_MODEL), lambda i, j: (i, 0)),
                  pl.BlockSpec((D_MODEL, tn), lambda i, j: (0, j)),
                  pl.BlockSpec((D_MODEL, tn), lambda i, j: (0, half + j)),
                  pl.BlockSpec((F_KERNEL, tn), lambda i, j: (0, j)),
                  pl.BlockSpec((F_KERNEL, tn), lambda i, j: (0, half + j))],
        out_specs=pl.BlockSpec((TM, tn), lambda i, j: (i, j)),
        out_shape=jax.ShapeDtypeStruct((TOK, D_FF), BF16),
        compiler_params=_params(2),
        name="ffn_up",
    )(h, w_up, w_up, conv_f, conv_f)


def _conv_a_kernel(ah_ref, ab_ref, ac_ref, w_ref, o_ref):
    i = pl.program_id(0)
    p = ac_ref[...].astype(F32) * ah_ref[...].astype(F32)
    pos, seq_len = _seq_pos(p.shape, i)
    o_ref[...] = (ab_ref[...].astype(F32) * _conv3_rows(p, w_ref[...], pos, seq_len)).astype(o_ref.dtype)


def _conv_a(z, conv_a):
    tn = 512
    col = lambda k: (lambda i, j: (i, (ZA + k * A_W) // tn + j))
    return pl.pallas_call(
        _conv_a_kernel,
        grid=(N_ROW_BLOCKS, A_W // tn),
        in_specs=[pl.BlockSpec((TM, tn), col(0)), pl.BlockSpec((TM, tn), col(1)),
                  pl.BlockSpec((TM, tn), col(2)),
                  pl.BlockSpec((A_KERNEL, tn), lambda i, j: (0, j))],
        out_specs=pl.BlockSpec((TM, tn), lambda i, j: (i, j)),
        out_shape=jax.ShapeDtypeStruct((TOK, A_W), BF16),
        compiler_params=_params(2),
        name="conv_a",
    )(z, z, z, conv_a)


def _conv_c_kernel(g1_ref, g2_ref, w_ref, lng_ref, lnb_ref, o_ref, u_ref, sh_ref):
    is_dec = pl.program_id(0) >= CTX_BLOCKS
    halo_w = SEQ + 2 * CONV_PAD

    def gated(r0, n):
        return g1_ref[r0:r0 + n, :].astype(F32) * _sigmoid(g2_ref[r0:r0 + n, :].astype(F32))

    zeros = jnp.zeros((CONV_PAD, C_W), F32)
    u_ref[halo_w:halo_w + SUBLANES, :] = zeros[:SUBLANES]

    for p in range(TM // SEQ):
        r0 = p * SEQ
        top = jnp.where(is_dec, gated(r0 - CONV_PAD, CONV_PAD), 0.0) if p > 0 else zeros
        bot = jnp.where(is_dec, gated(r0 + SEQ, CONV_PAD), 0.0) if p < TM // SEQ - 1 else zeros
        u_ref[0:CONV_PAD, :] = top
        u_ref[CONV_PAD:CONV_PAD + SEQ, :] = gated(r0, SEQ)
        u_ref[CONV_PAD + SEQ:halo_w, :] = bot

        def shift_rows(c, carry):
            base = pl.multiple_of(c * SUBLANES, SUBLANES)
            win = u_ref[pl.ds(base, 2 * SUBLANES), :]
            sh_ref[0, pl.ds(base, SUBLANES), :] = win[:SUBLANES]
            for b in range(1, SUBLANES):
                sh_ref[b, pl.ds(base, SUBLANES), :] = pltpu.roll(win, 2 * SUBLANES - b, 0)[:SUBLANES]
            return carry

        lax.fori_loop(0, halo_w // SUBLANES, shift_rows, 0)

        def chunk(c, carry):
            base = pl.multiple_of(c * CONV_ROWS, CONV_ROWS)
            acc = jnp.zeros((CONV_ROWS, C_W), F32)
            for k in range(C_KERNEL):
                off = CONV_PAD - C_KERNEL // 2 + k
                start = pl.multiple_of(base + (off // SUBLANES) * SUBLANES, SUBLANES)
                acc = acc + w_ref[k:k + 1, :] * sh_ref[off % SUBLANES, pl.ds(start, CONV_ROWS), :]
            mu = jnp.mean(acc, axis=-1, keepdims=True)
            d = acc - mu
            var = jnp.mean(d * d, axis=-1, keepdims=True)
            y = d * lax.rsqrt(var + EPS) * lng_ref[...] + lnb_ref[...]
            o_ref[pl.ds(r0 + base, CONV_ROWS), :] = _silu(y).astype(o_ref.dtype)
            return carry

        lax.fori_loop(0, SEQ // CONV_ROWS, chunk, 0)


def _conv_c(z, conv_c, ln_g, ln_b):
    halo_w = SEQ + 2 * CONV_PAD
    return pl.pallas_call(
        _conv_c_kernel,
        grid=(N_ROW_BLOCKS,),
        in_specs=[pl.BlockSpec((TM, C_W), lambda i: (i, ZGLU // C_W)),
                  pl.BlockSpec((TM, C_W), lambda i: (i, ZGLU // C_W + 1)),
                  pl.BlockSpec((C_KERNEL, C_W), lambda i: (0, 0)),
                  pl.BlockSpec((1, C_W), lambda i: (0, 0)),
                  pl.BlockSpec((1, C_W), lambda i: (0, 0))],
        out_specs=pl.BlockSpec((TM, C_W), lambda i: (i, 0)),
        out_shape=jax.ShapeDtypeStruct((TOK, C_W), BF16),
        scratch_shapes=[pltpu.VMEM((halo_w + SUBLANES, C_W), F32),
                        pltpu.VMEM((SUBLANES, halo_w, C_W), F32)],
        compiler_params=_params(1),
        name="conv_c",
    )(z, z, conv_c, ln_g, ln_b)


def _rope_pairs(x, cos, sin):
    half = ROPE_DIM // 4
    lane = lax.broadcasted_iota(jnp.int32, x.shape, 1)
    swapped = jnp.where((lane & (2 * half - 1)) < half,
                        pltpu.roll(x, LANES - half, 1), pltpu.roll(x, half, 1))
    return x * cos + swapped * sin


def _attn_kernel(*refs, rope, has_cache):
    refs = list(refs)
    qn_ref, qp_ref = refs[0:2]
    pos = 2
    if rope:
        cq_ref, sq_ref, ck_ref, sk_ref = refs[pos:pos + 4]
        pos += 4
    kv_ref, kpe_ref = refs[pos:pos + 2]
    pos += 2
    if has_cache:
        kvc_ref, kpec_ref = refs[pos:pos + 2]
        pos += 2
    o_ref = refs[pos]

    scale = QK_DIM ** -0.5
    qp = qp_ref[...]
    kpe = kpe_ref[...]
    if rope:
        qp = _rope_pairs(qp.astype(F32), cq_ref[...], sq_ref[...]).astype(BF16)
        kpe = _rope_pairs(kpe, ck_ref[...], sk_ref[...])

    def pe_keys(k):
        lane = lax.broadcasted_iota(jnp.int32, k.shape, 1)
        lo = jnp.where(lane < ROPE_DIM, k, 0.0)
        return lo.astype(BF16), pltpu.roll(lo, ROPE_DIM, 1).astype(BF16)

    segments = [(kv_ref, pe_keys(kpe))]
    if has_cache:
        segments.insert(0, (kvc_ref, pe_keys(kpec_ref[...])))

    for h in range(2):
        q = jnp.concatenate([qn_ref[:, h * NOPE_DIM:(h + 1) * NOPE_DIM], qp], axis=1)
        scores = []
        for seg_ref, pe in segments:
            base = h * (NOPE_DIM + V_DIM)
            k = jnp.concatenate([seg_ref[:, base:base + NOPE_DIM], pe[h]], axis=1)
            s = lax.dot_general(q, k, (((1,), (1,)), ((), ())), preferred_element_type=F32)
            scores.append(s * scale)
        m = scores[0].max(axis=-1, keepdims=True)
        for s in scores[1:]:
            m = jnp.maximum(m, s.max(axis=-1, keepdims=True))
        denom = 0.0
        out = 0.0
        for s, (seg_ref, _) in zip(scores, segments):
            base = h * (NOPE_DIM + V_DIM) + NOPE_DIM
            p = jnp.exp(s - m)
            denom = denom + p.sum(axis=-1, keepdims=True)
            out = out + jnp.dot(p.astype(BF16), seg_ref[:, base:base + V_DIM],
                                preferred_element_type=F32)
        o_ref[:, h * V_DIM:(h + 1) * V_DIM] = (out / denom).astype(o_ref.dtype)


def _attention(q, kv, kv_a, seq_len, n_seq, row0, tq, rope=None, cache=None):
    nq = seq_len // tq
    qblk0 = row0 // tq
    sblk0 = row0 // seq_len
    pair_w = 2 * NOPE_DIM
    pe_col0 = N_HEADS * NOPE_DIM // LANES
    in_specs = [pl.BlockSpec((tq, pair_w), lambda b, p, t: (qblk0 + b * nq + t, p)),
                pl.BlockSpec((tq, LANES), lambda b, p, t: (qblk0 + b * nq + t, pe_col0 + p))]
    args = [q, q]
    if rope is not None:
        cos, sin = rope
        in_specs += [pl.BlockSpec((tq, LANES), lambda b, p, t: (t, 0))] * 2
        in_specs += [pl.BlockSpec((seq_len, LANES), lambda b, p, t: (0, 0))] * 2
        args += [cos, sin, cos, sin]
    in_specs += [pl.BlockSpec((seq_len, 2 * pair_w), lambda b, p, t: (sblk0 + b, p)),
                 pl.BlockSpec((seq_len, LANES), lambda b, p, t: (sblk0 + b, KV_RANK // LANES))]
    args += [kv, kv_a]
    if cache is not None:
        kv_c, kpe_c = cache
        in_specs += [pl.BlockSpec((PAST_LEN, 2 * pair_w), lambda b, p, t: (b, p)),
                     pl.BlockSpec((None, PAST_LEN, LANES), lambda b, p, t: (b, 0, 0))]
        args += [kv_c, kpe_c]
    return pl.pallas_call(
        functools.partial(_attn_kernel, rope=rope is not None, has_cache=cache is not None),
        grid=(n_seq, HEAD_PAIRS, nq),
        in_specs=in_specs,
        out_specs=pl.BlockSpec((tq, pair_w), lambda b, p, t: (b * nq + t, p)),
        out_shape=jax.ShapeDtypeStruct((n_seq * seq_len, N_HEADS * V_DIM), BF16),
        compiler_params=_params(3),
        name="attention_dec" if cache is not None else "attention_ctx",
    )(*args)


def _rope_tables():
    rows = DEC_SEQ // GRID_W
    r = jnp.repeat(jnp.arange(rows), GRID_W).astype(F32)
    col = jnp.tile(jnp.arange(GRID_W), rows).astype(F32)
    half_axis = ROPE_DIM // 2
    inv = ROPE_THETA ** (-jnp.arange(0, half_axis, 2, dtype=F32) / half_axis)
    ar = r[:, None] * inv
    ac = col[:, None] * inv
    cos = jnp.concatenate([jnp.cos(ar), jnp.cos(ar), jnp.cos(ac), jnp.cos(ac)], axis=-1)
    sin = jnp.concatenate([-jnp.sin(ar), jnp.sin(ar), -jnp.sin(ac), jnp.sin(ac)], axis=-1)
    reps = LANES // ROPE_DIM
    return jnp.tile(cos, (1, reps)), jnp.tile(sin, (1, reps))


def kernel(x_prompt, x_sample, cache_ckv, cache_kpe, c, c_ctx, w_mod, b_mod, g_norm1, g_norm2, w_in, conv_a, w_a_out, g_q, w_uq, g_kv, w_ukv, w_mla_out, conv_c, ln_c_g, ln_c_b, w_c_out, w_o, w_up, conv_f, w_down, g_final):
    x = jnp.concatenate([x_prompt.reshape(CTX_TOK, D_MODEL), x_sample.reshape(DEC_TOK, D_MODEL)], axis=0)
    cond = jnp.concatenate([c_ctx[None, :], c, jnp.zeros((COND_ROWS - N_COND, D_MODEL), F32)], axis=0)
    mod = _modulation(cond, w_mod, b_mod)
    rope = _rope_tables()

    ckv_layers, kpe_layers = [], []
    for l in range(DEPTH):
        sh1, sc1, gt1, sh2, sc2, gt2 = [
            mod[l, :N_COND, k * D_MODEL:(k + 1) * D_MODEL].reshape(N_COND, 1, D_MODEL) for k in range(6)]
        wl = w_in[l]
        off = [0]
        for width in (A_W, A_W, A_W, Q_RANK, KV_RANK + ROPE_DIM, 2 * C_W, D_MODEL, D_MODEL, D_MODEL):
            off.append(off[-1] + width)
        w_main = jnp.concatenate(
            [wl[:, off[3]:off[4]], jnp.zeros((D_MODEL, ZGATE - Q_RANK), F32),
             wl[:, off[6]:off[9]], wl[:, off[0]:off[3]], wl[:, off[5]:off[6]]], axis=1).astype(BF16)
        w_kv_a = jnp.concatenate(
            [wl[:, off[4]:off[5]], jnp.zeros((D_MODEL, KV_A_W - KV_RANK - ROPE_DIM), F32)], axis=1).astype(BF16)
        wq = w_uq[l].reshape(Q_RANK, N_HEADS, QK_DIM)
        w_q = jnp.concatenate([wq[:, :, :NOPE_DIM].reshape(Q_RANK, N_HEADS * NOPE_DIM),
                               wq[:, :, NOPE_DIM:].reshape(Q_RANK, N_HEADS * ROPE_DIM)], axis=1).astype(BF16)
        w_kv = w_ukv[l].astype(BF16)

        h = _norm_mod(x, g_norm1[l][None, :], sc1, sh1)
        z = _matmul(h, w_main, BF16, tn=512, name="in_proj")
        kv_a = _matmul(h, w_kv_a, F32, tn=KV_A_W, name="kv_a_proj")
        ya = _conv_a(z, conv_a[l])
        q = _q_proj(z, g_q[l][None, :], w_q)
        ckv, kv = _kv_proj(kv_a, g_kv[l][None, :], w_kv)
        kv_cache = _matmul(cache_ckv[:, l].reshape(DEC_BATCH * PAST_LEN, KV_RANK).astype(BF16), w_kv, BF16,
                           tn=1024, tm=DEC_BATCH * PAST_LEN, name="cache_kv_proj")
        kpe_cache = jnp.pad(cache_kpe[:, l], ((0, 0), (0, 0), (0, LANES - ROPE_DIM)))
        attn_ctx = _attention(q, kv, kv_a, SEQ, BATCH, 0, SEQ)
        attn_dec = _attention(q, kv, kv_a, DEC_SEQ, DEC_BATCH, CTX_TOK, 256, rope=rope,
                              cache=(kv_cache, kpe_cache))
        attn = jnp.concatenate([attn_ctx, attn_dec], axis=0)
        yc = _conv_c(z, conv_c[l], ln_c_g[l][None, :], ln_c_b[l][None, :])
        merged = _merge(ya, w_a_out[l].astype(BF16), attn, w_mla_out[l].astype(BF16), yc,
                        w_c_out[l].astype(BF16), z)
        x = _matmul_residual(merged, w_o[l].astype(BF16), x, gt1, tn=512, name="out_proj")

        h = _norm_mod(x, g_norm2[l][None, :], sc2, sh2)
        act = _ffn_up(h, w_up[l].astype(BF16), conv_f[l])
        x = _matmul_residual(act, w_down[l].astype(BF16), x, gt2, tn=256, name="ffn_down")

        ckv_layers.append(ckv[:CTX_TOK].reshape(BATCH, SEQ, KV_RANK))
        kpe_layers.append(kv_a[:CTX_TOK, KV_RANK:KV_RANK + ROPE_DIM].reshape(BATCH, SEQ, ROPE_DIM))

    y = _final_norm(x, g_final[None, :])
    y_prompt = y[:CTX_TOK].reshape(BATCH, SEQ, D_MODEL)
    y_sample = y[CTX_TOK:].reshape(DEC_BATCH, DEC_SEQ, D_MODEL)
    return (y_prompt, y_sample, jnp.stack(ckv_layers, axis=1), jnp.stack(kpe_layers, axis=1))
```

```python
import functools
import math

import jax
import jax.numpy as jnp
from jax import lax
from jax.experimental import pallas as pl
from jax.experimental.pallas import tpu as pltpu

D_MODEL = 2048
BATCH = 16
SEQ = 256
DEPTH = 2
DEC_BATCH = 2
DEC_SEQ = 1024
PAST_LEN = 256
GRID_W = 64
A_W = 1024
A_KERNEL = 3
N_HEADS = 16
Q_RANK = 768
KV_RANK = 512
NOPE_DIM = 128
ROPE_DIM = 64
V_DIM = 128
QK_DIM = NOPE_DIM + ROPE_DIM
C_W = 1024
C_KERNEL = 31
D_FF = 5632
F_KERNEL = 3
ROPE_THETA = 10000.0
EPS = 1e-6

CTX_TOK = BATCH * SEQ
DEC_TOK = DEC_BATCH * DEC_SEQ
TOK = CTX_TOK + DEC_TOK
TM = 1024
N_ROW_BLOCKS = TOK // TM
CTX_BLOCKS = CTX_TOK // TM
N_COND = 3
COND_ROWS = 8
LANES = 128
SUBLANES = 8
HEAD_PAIRS = N_HEADS // 2
KV_A_W = 640
CONV_PAD = 16
CONV_ROWS = 64
NORM_ROWS = 64
TQ = 256
VMEM_LIMIT = 56 * 1024 * 1024

IN_AH, IN_AB, IN_AC = 0, A_W, 2 * A_W
IN_Q = 3 * A_W
IN_KV = IN_Q + Q_RANK
IN_GLU = IN_KV + KV_RANK + ROPE_DIM
TAIL_G1, TAIL_G2, TAIL_GATE = 0, C_W, 2 * C_W
TAIL_W = 2 * C_W + 3 * D_MODEL

F32 = jnp.float32
BF16 = jnp.bfloat16


def _params(n_axes):
    return pltpu.CompilerParams(dimension_semantics=("arbitrary",) * n_axes,
                                vmem_limit_bytes=VMEM_LIMIT)


def _cond_row(i):
    return jnp.maximum(i - (CTX_BLOCKS - 1), 0)


def _sigmoid(x):
    return jax.nn.sigmoid(x)


def _silu(x):
    return x * jax.nn.sigmoid(x)


def _bdot(a, w):
    return jnp.dot(a, w.astype(BF16), preferred_element_type=F32)


def _seq_pos(shape, i):
    seq_len = jnp.where(i < CTX_BLOCKS, SEQ, DEC_SEQ)
    row = lax.broadcasted_iota(jnp.int32, shape, 0)
    return row & (seq_len - 1), seq_len


def _conv3_rows(u, w, pos, seq_len):
    n = u.shape[0]
    prev = jnp.where(pos == 0, 0.0, pltpu.roll(u, 1, 0))
    nxt = jnp.where(pos == seq_len - 1, 0.0, pltpu.roll(u, n - 1, 0))
    return w[0:1, :] * prev + w[1:2, :] * u + w[2:3, :] * nxt


def _mod_kernel(c_ref, w_ref, b_ref, o_ref):
    s = _silu(c_ref[...]).astype(BF16)
    o_ref[...] = _bdot(s, w_ref[...]) + b_ref[...]


def _modulation(cond, w_mod, b_mod):
    tn = 1024
    n = 6 * D_MODEL
    return pl.pallas_call(
        _mod_kernel,
        grid=(DEPTH, n // tn),
        in_specs=[pl.BlockSpec((COND_ROWS, D_MODEL), lambda l, j: (0, 0)),
                  pl.BlockSpec((None, D_MODEL, tn), lambda l, j: (l, 0, j)),
                  pl.BlockSpec((None, 1, tn), lambda l, j: (l, 0, j))],
        out_specs=pl.BlockSpec((None, COND_ROWS, tn), lambda l, j: (l, 0, j)),
        out_shape=jax.ShapeDtypeStruct((DEPTH, COND_ROWS, n), F32),
        compiler_params=_params(2),
        name="modulation",
    )(cond, w_mod, b_mod.reshape(DEPTH, 1, n))


def _rms(x):
    return x * lax.rsqrt(jnp.mean(x * x, axis=-1, keepdims=True) + EPS)


def _norm_mod_kernel(x_ref, g_ref, sc_ref, sh_ref, o_ref):
    y = _rms(x_ref[...]) * g_ref[...]
    o_ref[...] = (y * (1.0 + sc_ref[...]) + sh_ref[...]).astype(o_ref.dtype)


def _norm_mod(x, g, sc, sh):
    tr = 512
    per = TM // tr
    cond = lambda i: (_cond_row(i // per), 0, 0)
    return pl.pallas_call(
        _norm_mod_kernel,
        grid=(TOK // tr,),
        in_specs=[pl.BlockSpec((tr, D_MODEL), lambda i: (i, 0)),
                  pl.BlockSpec((1, D_MODEL), lambda i: (0, 0)),
                  pl.BlockSpec((None, 1, D_MODEL), cond),
                  pl.BlockSpec((None, 1, D_MODEL), cond)],
        out_specs=pl.BlockSpec((tr, D_MODEL), lambda i: (i, 0)),
        out_shape=jax.ShapeDtypeStruct((TOK, D_MODEL), BF16),
        compiler_params=_params(1),
        name="norm_mod",
    )(x, g, sc, sh)


def _final_norm_kernel(x_ref, g_ref, o_ref):
    o_ref[...] = _rms(x_ref[...]) * g_ref[...]


def _final_norm(x, g, row0, rows):
    tr = 512
    return pl.pallas_call(
        _final_norm_kernel,
        grid=(rows // tr,),
        in_specs=[pl.BlockSpec((tr, D_MODEL), lambda i: (row0 // tr + i, 0)),
                  pl.BlockSpec((1, D_MODEL), lambda i: (0, 0))],
        out_specs=pl.BlockSpec((tr, D_MODEL), lambda i: (i, 0)),
        out_shape=jax.ShapeDtypeStruct((rows, D_MODEL), F32),
        compiler_params=_params(1),
        name="final_norm",
    )(x, g)


def _mm_kernel(a_ref, w_ref, o_ref):
    o_ref[...] = _bdot(a_ref[...], w_ref[...]).astype(o_ref.dtype)


def _matmul(a, w, out_dtype, tn, tm=TM, name="matmul"):
    m, k = a.shape
    n = w.shape[1]
    return pl.pallas_call(
        _mm_kernel,
        grid=(m // tm, n // tn),
        in_specs=[pl.BlockSpec((tm, k), lambda i, j: (i, 0)),
                  pl.BlockSpec((k, tn), lambda i, j: (0, j))],
        out_specs=pl.BlockSpec((tm, tn), lambda i, j: (i, j)),
        out_shape=jax.ShapeDtypeStruct((m, n), out_dtype),
        compiler_params=_params(2),
        name=name,
    )(a, w)


def _mm_residual_kernel(a_ref, w_ref, x_ref, gt_ref, o_ref):
    o_ref[...] = x_ref[...] + gt_ref[...] * _bdot(a_ref[...], w_ref[...])


def _matmul_residual(a, w, l, x, gt, tn, name):
    m, k = a.shape
    n = w.shape[2]
    return pl.pallas_call(
        _mm_residual_kernel,
        grid=(m // TM, n // tn),
        in_specs=[pl.BlockSpec((TM, k), lambda i, j: (i, 0)),
                  pl.BlockSpec((None, k, tn), lambda i, j: (l, 0, j)),
                  pl.BlockSpec((TM, tn), lambda i, j: (i, j)),
                  pl.BlockSpec((None, 1, tn), lambda i, j: (_cond_row(i), 0, j))],
        out_specs=pl.BlockSpec((TM, tn), lambda i, j: (i, j)),
        out_shape=jax.ShapeDtypeStruct((m, n), F32),
        compiler_params=_params(2),
        name=name,
    )(a, w, x, gt)


def _mixer_a_kernel(h_ref, wh_ref, wb_ref, wc_ref, cw_ref, o_ref):
    i = pl.program_id(0)
    h = h_ref[...]
    p = _bdot(h, wc_ref[...]) * _bdot(h, wh_ref[...])
    pos, seq_len = _seq_pos(p.shape, i)
    o_ref[...] = (_bdot(h, wb_ref[...]) * _conv3_rows(p, cw_ref[...], pos, seq_len)).astype(o_ref.dtype)


def _mixer_a(h, w_in, conv_a, l):
    tn = 256
    col = lambda off: (lambda i, j: (l, 0, off // tn + j))
    wspec = lambda off: pl.BlockSpec((None, D_MODEL, tn), col(off))
    return pl.pallas_call(
        _mixer_a_kernel,
        grid=(N_ROW_BLOCKS, A_W // tn),
        in_specs=[pl.BlockSpec((TM, D_MODEL), lambda i, j: (i, 0)),
                  wspec(IN_AH), wspec(IN_AB), wspec(IN_AC),
                  pl.BlockSpec((None, A_KERNEL, tn), lambda i, j: (l, 0, j))],
        out_specs=pl.BlockSpec((TM, tn), lambda i, j: (i, j)),
        out_shape=jax.ShapeDtypeStruct((TOK, A_W), BF16),
        compiler_params=_params(2),
        name="mixer_a",
    )(h, w_in, w_in, w_in, conv_a)


def _q_kernel(h_ref, wa_ref, g_ref, w_ref, o_ref, qn_ref):
    @pl.when(pl.program_id(1) == 0)
    def _():
        qn_ref[...] = (_rms(_bdot(h_ref[...], wa_ref[...])) * g_ref[...]).astype(BF16)

    o_ref[...] = _bdot(qn_ref[...], w_ref[...]).astype(o_ref.dtype)


def _q_proj(h, w_in, g_q, w_q, l):
    tn = 1024
    n = w_q.shape[1]
    return pl.pallas_call(
        _q_kernel,
        grid=(N_ROW_BLOCKS, n // tn),
        in_specs=[pl.BlockSpec((TM, D_MODEL), lambda i, j: (i, 0)),
                  pl.BlockSpec((None, D_MODEL, Q_RANK), lambda i, j: (l, 0, IN_Q // Q_RANK)),
                  pl.BlockSpec((1, Q_RANK), lambda i, j: (0, 0)),
                  pl.BlockSpec((Q_RANK, tn), lambda i, j: (0, j))],
        out_specs=pl.BlockSpec((TM, tn), lambda i, j: (i, j)),
        out_shape=jax.ShapeDtypeStruct((TOK, n), BF16),
        scratch_shapes=[pltpu.VMEM((TM, Q_RANK), BF16)],
        compiler_params=_params(2),
        name="q_proj",
    )(h, w_in, g_q, w_q)


def _kv_kernel(h_ref, wa_ref, g_ref, w_ref, ckv_ref, kpe_ref, kv_ref, cn_ref):
    @pl.when(pl.program_id(1) == 0)
    def _():
        kv_a = _bdot(h_ref[...], wa_ref[...])
        y = _rms(kv_a[:, :KV_RANK]) * g_ref[...]
        ckv_ref[...] = y
        kpe_ref[...] = kv_a[:, KV_RANK:]
        cn_ref[...] = y.astype(BF16)

    kv_ref[...] = _bdot(cn_ref[...], w_ref[...]).astype(kv_ref.dtype)


def _kv_proj(h, w_in, g_kv, w_ukv, l):
    tn = 1024
    n = w_ukv.shape[2]
    return pl.pallas_call(
        _kv_kernel,
        grid=(N_ROW_BLOCKS, n // tn),
        in_specs=[pl.BlockSpec((TM, D_MODEL), lambda i, j: (i, 0)),
                  pl.BlockSpec((None, D_MODEL, KV_A_W), lambda i, j: (l, 0, IN_KV // KV_A_W)),
                  pl.BlockSpec((1, KV_RANK), lambda i, j: (0, 0)),
                  pl.BlockSpec((None, KV_RANK, tn), lambda i, j: (l, 0, j))],
        out_specs=[pl.BlockSpec((TM, KV_RANK), lambda i, j: (i, 0)),
                   pl.BlockSpec((TM, LANES), lambda i, j: (i, 0)),
                   pl.BlockSpec((TM, tn), lambda i, j: (i, j))],
        out_shape=[jax.ShapeDtypeStruct((TOK, KV_RANK), F32),
                   jax.ShapeDtypeStruct((TOK, LANES), F32),
                   jax.ShapeDtypeStruct((TOK, n), BF16)],
        scratch_shapes=[pltpu.VMEM((TM, KV_RANK), BF16)],
        compiler_params=_params(2),
        name="kv_proj",
    )(h, w_in, g_kv, w_ukv)


def _cache_kv_kernel(a_ref, w_ref, o_ref):
    o_ref[...] = _bdot(a_ref[...].astype(BF16), w_ref[...]).astype(o_ref.dtype)


def _cache_kv_proj(cache_ckv, w_ukv, l):
    tn = 1024
    n = w_ukv.shape[2]
    return pl.pallas_call(
        _cache_kv_kernel,
        grid=(DEC_BATCH, n // tn),
        in_specs=[pl.BlockSpec((None, None, PAST_LEN, KV_RANK), lambda b, j: (b, l, 0, 0)),
                  pl.BlockSpec((None, KV_RANK, tn), lambda b, j: (l, 0, j))],
        out_specs=pl.BlockSpec((PAST_LEN, tn), lambda b, j: (b, j)),
        out_shape=jax.ShapeDtypeStruct((DEC_BATCH * PAST_LEN, n), BF16),
        compiler_params=_params(2),
        name="cache_kv_proj",
    )(cache_ckv, w_ukv)


def _merge_kernel(a1_ref, w1_ref, a2_ref, w2_ref, a3_ref, w3_ref, g1_ref, g2_ref, g3_ref, o_ref):
    def branch(a_ref, w_ref, g_ref):
        return _sigmoid(g_ref[...].astype(F32)) * _bdot(a_ref[...], w_ref[...])

    o_ref[...] = (branch(a1_ref, w1_ref, g1_ref) + branch(a2_ref, w2_ref, g2_ref)
                  + branch(a3_ref, w3_ref, g3_ref)).astype(o_ref.dtype)


def _merge(ya, w_a_out, attn, w_mla_out, yc, w_c_out, tail, l):
    tn = 256
    gate = lambda k: (lambda i, j: (i, (TAIL_GATE + k * D_MODEL) // tn + j))
    act = lambda width: pl.BlockSpec((TM, width), lambda i, j: (i, 0))
    wgt = lambda width: pl.BlockSpec((None, width, tn), lambda i, j: (l, 0, j))
    return pl.pallas_call(
        _merge_kernel,
        grid=(N_ROW_BLOCKS, D_MODEL // tn),
        in_specs=[act(A_W), wgt(A_W), act(N_HEADS * V_DIM), wgt(N_HEADS * V_DIM), act(C_W), wgt(C_W),
                  pl.BlockSpec((TM, tn), gate(0)), pl.BlockSpec((TM, tn), gate(1)),
                  pl.BlockSpec((TM, tn), gate(2))],
        out_specs=pl.BlockSpec((TM, tn), lambda i, j: (i, j)),
        out_shape=jax.ShapeDtypeStruct((TOK, D_MODEL), BF16),
        compiler_params=_params(2),
        name="merge",
    )(ya, w_a_out, attn, w_mla_out, yc, w_c_out, tail, tail, tail)


def _ffn_up_kernel(h_ref, wa_ref, wg_ref, ca_ref, cg_ref, o_ref):
    i = pl.program_id(0)
    h = h_ref[...]
    ua = _bdot(h, wa_ref[...])
    ug = _bdot(h, wg_ref[...])
    pos, seq_len = _seq_pos(ua.shape, i)
    a = _conv3_rows(ua, ca_ref[...], pos, seq_len)
    g = _conv3_rows(ug, cg_ref[...], pos, seq_len)
    o_ref[...] = (_silu(a) * g).astype(o_ref.dtype)


def _ffn_up(h, w_up, conv_f, l):
    tn = 512
    half = D_FF // tn
    return pl.pallas_call(
        _ffn_up_kernel,
        grid=(N_ROW_BLOCKS, half),
        in_specs=[pl.BlockSpec((TM, D_MODEL), lambda i, j: (i, 0)),
                  pl.BlockSpec((None, D_MODEL, tn), lambda i, j: (l, 0, j)),
                  pl.BlockSpec((None, D_MODEL, tn), lambda i, j: (l, 0, half + j)),
                  pl.BlockSpec((None, F_KERNEL, tn), lambda i, j: (l, 0, j)),
                  pl.BlockSpec((None, F_KERNEL, tn), lambda i, j: (l, 0, half + j))],
        out_specs=pl.BlockSpec((TM, tn), lambda i, j: (i, j)),
        out_shape=jax.ShapeDtypeStruct((TOK, D_FF), BF16),
        compiler_params=_params(2),
        name="ffn_up",
    )(h, w_up, w_up, conv_f, conv_f)


def _conv_c_kernel(g1_ref, g2_ref, w_ref, lng_ref, lnb_ref, o_ref, u_ref, sh_ref, cv_ref):
    is_dec = pl.program_id(0) >= CTX_BLOCKS
    halo_w = SEQ + 2 * CONV_PAD
    n_pieces = TM // SEQ

    def gated(r0, n):
        return g1_ref[r0:r0 + n, :].astype(F32) * _sigmoid(g2_ref[r0:r0 + n, :].astype(F32))

    zeros = jnp.zeros((CONV_PAD, C_W), F32)
    u_ref[halo_w:halo_w + SUBLANES, :] = zeros[:SUBLANES]

    for p in range(n_pieces):
        r0 = p * SEQ
        top = jnp.where(is_dec, gated(r0 - CONV_PAD, CONV_PAD), 0.0) if p > 0 else zeros
        bot = jnp.where(is_dec, gated(r0 + SEQ, CONV_PAD), 0.0) if p < n_pieces - 1 else zeros
        u_ref[0:CONV_PAD, :] = top
        u_ref[CONV_PAD:CONV_PAD + SEQ, :] = gated(r0, SEQ)
        u_ref[CONV_PAD + SEQ:halo_w, :] = bot

        def shift_rows(c, carry):
            base = pl.multiple_of(c * SUBLANES, SUBLANES)
            win = u_ref[pl.ds(base, 2 * SUBLANES), :]
            sh_ref[0, pl.ds(base, SUBLANES), :] = win[:SUBLANES]
            for b in range(1, SUBLANES):
                sh_ref[b, pl.ds(base, SUBLANES), :] = pltpu.roll(win, 2 * SUBLANES - b, 0)[:SUBLANES]
            return carry

        lax.fori_loop(0, halo_w // SUBLANES, shift_rows, 0)

        for s in range(C_W // LANES):
            cols = slice(s * LANES, (s + 1) * LANES)
            taps = [jnp.broadcast_to(w_ref[k:k + 1, cols], (SUBLANES, LANES)) for k in range(C_KERNEL)]

            def conv_rows(c, carry, cols=cols, taps=taps):
                base = pl.multiple_of(c * CONV_ROWS, CONV_ROWS)
                for r in range(0, CONV_ROWS, SUBLANES):
                    acc = None
                    for k in range(C_KERNEL):
                        off = CONV_PAD - C_KERNEL // 2 + k
                        start = pl.multiple_of(base + r + (off // SUBLANES) * SUBLANES, SUBLANES)
                        term = taps[k] * sh_ref[off % SUBLANES, pl.ds(start, SUBLANES), cols]
                        acc = term if acc is None else acc + term
                    cv_ref[pl.ds(base + r, SUBLANES), cols] = acc
                return carry

            lax.fori_loop(0, SEQ // CONV_ROWS, conv_rows, 0)

        for base in range(0, SEQ, NORM_ROWS):
            v = cv_ref[base:base + NORM_ROWS, :]
            d = v - jnp.mean(v, axis=-1, keepdims=True)
            var = jnp.mean(d * d, axis=-1, keepdims=True)
            y = d * lax.rsqrt(var + EPS) * lng_ref[...] + lnb_ref[...]
            o_ref[r0 + base:r0 + base + NORM_ROWS, :] = _silu(y).astype(o_ref.dtype)


def _conv_c(tail, conv_c, ln_g, ln_b, l):
    halo_w = SEQ + 2 * CONV_PAD
    return pl.pallas_call(
        _conv_c_kernel,
        grid=(N_ROW_BLOCKS,),
        in_specs=[pl.BlockSpec((TM, C_W), lambda i: (i, TAIL_G1 // C_W)),
                  pl.BlockSpec((TM, C_W), lambda i: (i, TAIL_G2 // C_W)),
                  pl.BlockSpec((None, C_KERNEL, C_W), lambda i: (l, 0, 0)),
                  pl.BlockSpec((1, C_W), lambda i: (0, 0)),
                  pl.BlockSpec((1, C_W), lambda i: (0, 0))],
        out_specs=pl.BlockSpec((TM, C_W), lambda i: (i, 0)),
        out_shape=jax.ShapeDtypeStruct((TOK, C_W), BF16),
        scratch_shapes=[pltpu.VMEM((halo_w + SUBLANES, C_W), F32),
                        pltpu.VMEM((SUBLANES, halo_w, C_W), F32),
                        pltpu.VMEM((SEQ, C_W), F32)],
        compiler_params=_params(1),
        name="conv_c",
    )(tail, tail, conv_c, ln_g, ln_b)


def _rope_pairs(x, cos, sin):
    half = ROPE_DIM // 4
    lane = lax.broadcasted_iota(jnp.int32, x.shape, 1)
    swapped = jnp.where((lane & (2 * half - 1)) < half,
                        pltpu.roll(x, LANES - half, 1), pltpu.roll(x, half, 1))
    return x * cos + swapped * sin


def _pe_keys(k):
    lane = lax.broadcasted_iota(jnp.int32, k.shape, 1)
    lo = jnp.where(lane < ROPE_DIM, k, 0.0)
    return lo.astype(BF16), pltpu.roll(lo, ROPE_DIM, 1).astype(BF16)


def _attend(q_ref, segments, o_ref, rope_q=None):
    c = QK_DIM ** -0.5 * math.log2(math.e)
    pe_col0 = N_HEADS * NOPE_DIM
    for hp in range(HEAD_PAIRS):
        qp = q_ref[:, pe_col0 + hp * LANES:pe_col0 + (hp + 1) * LANES]
        if rope_q is not None:
            qp = _rope_pairs(qp.astype(F32), *rope_q).astype(BF16)
        for h in range(2):
            head = 2 * hp + h
            q = jnp.concatenate([q_ref[:, head * NOPE_DIM:(head + 1) * NOPE_DIM], qp], axis=1)
            kbase = head * (NOPE_DIM + V_DIM)
            scores = []
            for kv_ref, pe in segments:
                k = jnp.concatenate([kv_ref[:, kbase:kbase + NOPE_DIM], pe[h]], axis=1)
                scores.append(lax.dot_general(q, k, (((1,), (1,)), ((), ())),
                                              preferred_element_type=F32))
            m = scores[0].max(axis=-1, keepdims=True)
            for s in scores[1:]:
                m = jnp.maximum(m, s.max(axis=-1, keepdims=True))
            denom = 0.0
            out = 0.0
            for s, (kv_ref, _) in zip(scores, segments):
                p = jnp.exp2((s - m) * c)
                denom = denom + p.sum(axis=-1, keepdims=True)
                out = out + jnp.dot(p.astype(BF16), kv_ref[:, kbase + NOPE_DIM:kbase + NOPE_DIM + V_DIM],
                                    preferred_element_type=F32)
            o_ref[:, head * V_DIM:(head + 1) * V_DIM] = (out / denom).astype(o_ref.dtype)


def _attn_kernel(q_ref, kvx_ref, kpex_ref, kvd_ref, kped_ref, kvc_ref, kpec_ref,
                 cq_ref, sq_ref, ck_ref, sk_ref, o_ref):
    step = pl.program_id(0)

    @pl.when(step < BATCH)
    def _():
        _attend(q_ref, [(kvx_ref, _pe_keys(kpex_ref[...]))], o_ref)

    @pl.when(step >= BATCH)
    def _():
        new_pe = _rope_pairs(kped_ref[...], ck_ref[...], sk_ref[...])
        _attend(q_ref, [(kvc_ref, _pe_keys(kpec_ref[...])), (kvd_ref, _pe_keys(new_pe))], o_ref,
                rope_q=(cq_ref[...], sq_ref[...]))


def _attention(q, kv, kpe, kv_cache, kpe_cache, cos, sin):
    per_seq = DEC_SEQ // TQ
    n_steps = BATCH + DEC_BATCH * per_seq
    ctx = lambda s: jnp.minimum(s, BATCH - 1)
    dec = lambda s: jnp.maximum(s - BATCH, 0)
    kv_w = N_HEADS * (NOPE_DIM + V_DIM)
    in_specs = [
        pl.BlockSpec((TQ, q.shape[1]), lambda s: (s, 0)),
        pl.BlockSpec((SEQ, kv_w), lambda s: (ctx(s), 0)),
        pl.BlockSpec((SEQ, LANES), lambda s: (ctx(s), 0)),
        pl.BlockSpec((DEC_SEQ, kv_w), lambda s: (CTX_TOK // DEC_SEQ + dec(s) // per_seq, 0)),
        pl.BlockSpec((DEC_SEQ, LANES), lambda s: (CTX_TOK // DEC_SEQ + dec(s) // per_seq, 0)),
        pl.BlockSpec((PAST_LEN, kv_w), lambda s: (dec(s) // per_seq, 0)),
        pl.BlockSpec((None, PAST_LEN, LANES), lambda s: (dec(s) // per_seq, 0, 0)),
        pl.BlockSpec((TQ, LANES), lambda s: (dec(s) % per_seq, 0)),
        pl.BlockSpec((TQ, LANES), lambda s: (dec(s) % per_seq, 0)),
        pl.BlockSpec((DEC_SEQ, LANES), lambda s: (0, 0)),
        pl.BlockSpec((DEC_SEQ, LANES), lambda s: (0, 0)),
    ]
    return pl.pallas_call(
        _attn_kernel,
        grid=(n_steps,),
        in_specs=in_specs,
        out_specs=pl.BlockSpec((TQ, N_HEADS * V_DIM), lambda s: (s, 0)),
        out_shape=jax.ShapeDtypeStruct((TOK, N_HEADS * V_DIM), BF16),
        compiler_params=_params(1),
        name="attention",
    )(q, kv, kpe, kv, kpe, kv_cache, kpe_cache, cos, sin, cos, sin)


def _rope_tables():
    rows = DEC_SEQ // GRID_W
    r = jnp.repeat(jnp.arange(rows), GRID_W).astype(F32)
    col = jnp.tile(jnp.arange(GRID_W), rows).astype(F32)
    half_axis = ROPE_DIM // 2
    inv = ROPE_THETA ** (-jnp.arange(0, half_axis, 2, dtype=F32) / half_axis)
    ar = r[:, None] * inv
    ac = col[:, None] * inv
    cos = jnp.concatenate([jnp.cos(ar), jnp.cos(ar), jnp.cos(ac), jnp.cos(ac)], axis=-1)
    sin = jnp.concatenate([-jnp.sin(ar), jnp.sin(ar), -jnp.sin(ac), jnp.sin(ac)], axis=-1)
    reps = LANES // ROPE_DIM
    return jnp.tile(cos, (1, reps)), jnp.tile(sin, (1, reps))


def kernel(x_prompt, x_sample, cache_ckv, cache_kpe, c, c_ctx, w_mod, b_mod, g_norm1, g_norm2, w_in, conv_a, w_a_out, g_q, w_uq, g_kv, w_ukv, w_mla_out, conv_c, ln_c_g, ln_c_b, w_c_out, w_o, w_up, conv_f, w_down, g_final):
    x = jnp.concatenate([x_prompt.reshape(CTX_TOK, D_MODEL), x_sample.reshape(DEC_TOK, D_MODEL)], axis=0)
    cond = jnp.concatenate([c_ctx[None, :], c, jnp.zeros((COND_ROWS - N_COND, D_MODEL), F32)], axis=0)
    mod = _modulation(cond, w_mod, b_mod)
    cos, sin = _rope_tables()

    w_tail = w_in[:, :, IN_GLU:].astype(BF16)
    wq = w_uq.reshape(DEPTH, Q_RANK, N_HEADS, QK_DIM)
    w_q = jnp.concatenate([wq[..., :NOPE_DIM].reshape(DEPTH, Q_RANK, N_HEADS * NOPE_DIM),
                           wq[..., NOPE_DIM:].reshape(DEPTH, Q_RANK, N_HEADS * ROPE_DIM)],
                          axis=2).astype(BF16)
    kpe_cache = jnp.pad(cache_kpe, ((0, 0), (0, 0), (0, 0), (0, LANES - ROPE_DIM)))

    ckv_layers, kpe_layers = [], []
    for l in range(DEPTH):
        sh1, sc1, gt1, sh2, sc2, gt2 = [
            mod[l, :N_COND, k * D_MODEL:(k + 1) * D_MODEL].reshape(N_COND, 1, D_MODEL) for k in range(6)]

        h = _norm_mod(x, g_norm1[l][None, :], sc1, sh1)
        ya = _mixer_a(h, w_in, conv_a, l)
        q = _q_proj(h, w_in, g_q[l][None, :], w_q[l], l)
        ckv, kpe, kv = _kv_proj(h, w_in, g_kv[l][None, :], w_ukv, l)
        tail = _matmul(h, w_tail[l], BF16, tn=512, name="tail_proj")
        kv_cache = _cache_kv_proj(cache_ckv, w_ukv, l)
        attn = _attention(q, kv, kpe, kv_cache, kpe_cache[:, l], cos, sin)
        yc = _conv_c(tail, conv_c, ln_c_g[l][None, :], ln_c_b[l][None, :], l)
        merged = _merge(ya, w_a_out, attn, w_mla_out, yc, w_c_out, tail, l)
        x = _matmul_residual(merged, w_o, l, x, gt1, tn=512, name="out_proj")

        h = _norm_mod(x, g_norm2[l][None, :], sc2, sh2)
        act = _ffn_up(h, w_up, conv_f, l)
        x = _matmul_residual(act, w_down, l, x, gt2, tn=256, name="ffn_down")

        ckv_layers.append(ckv[:CTX_TOK].reshape(BATCH, SEQ, KV_RANK))
        kpe_layers.append(kpe[:CTX_TOK, :ROPE_DIM].reshape(BATCH, SEQ, ROPE_DIM))

    g = g_final[None, :]
    y_prompt = _final_norm(x, g, 0, CTX_TOK).reshape(BATCH, SEQ, D_MODEL)
    y_sample = _final_norm(x, g, CTX_TOK, DEC_TOK).reshape(DEC_BATCH, DEC_SEQ, D_MODEL)
    return (y_prompt, y_sample, jnp.stack(ckv_layers, axis=1), jnp.stack(kpe_layers, axis=1))
```

```python
import math

import jax
import jax.numpy as jnp
from jax import lax
from jax.experimental import pallas as pl
from jax.experimental.pallas import tpu as pltpu

D_MODEL = 2048
BATCH = 16
SEQ = 256
DEPTH = 2
DEC_BATCH = 2
DEC_SEQ = 1024
PAST_LEN = 256
GRID_W = 64
A_W = 1024
A_KERNEL = 3
N_HEADS = 16
Q_RANK = 768
KV_RANK = 512
NOPE_DIM = 128
ROPE_DIM = 64
V_DIM = 128
QK_DIM = NOPE_DIM + ROPE_DIM
C_W = 1024
C_KERNEL = 31
D_FF = 5632
F_KERNEL = 3
ROPE_THETA = 10000.0
EPS = 1e-6

CTX_TOK = BATCH * SEQ
DEC_TOK = DEC_BATCH * DEC_SEQ
TOK = CTX_TOK + DEC_TOK
TM = 1024
N_ROW_BLOCKS = TOK // TM
CTX_BLOCKS = CTX_TOK // TM
TMW = 2048
N_COND = 3
COND_ROWS = 8
LANES = 128
SUBLANES = 8
HEAD_PAIRS = N_HEADS // 2
KV_A_W = 640
CONV_PAD = 16
CONV_ROWS = 64
NORM_ROWS = 64
TQ = 256
VMEM_LIMIT = 56 * 1024 * 1024

IN_AH, IN_AB, IN_AC = 0, A_W, 2 * A_W
IN_Q = 3 * A_W
IN_KV = IN_Q + Q_RANK
IN_GLU = IN_KV + KV_RANK + ROPE_DIM
TAIL_G1, TAIL_G2, TAIL_GATE = 0, C_W, 2 * C_W
TAIL_W = 2 * C_W + 3 * D_MODEL

F32 = jnp.float32
BF16 = jnp.bfloat16


def _params(n_axes):
    return pltpu.CompilerParams(dimension_semantics=("arbitrary",) * n_axes,
                                vmem_limit_bytes=VMEM_LIMIT)


def _cond_of_row(row):
    return jnp.maximum(row // DEC_SEQ - (CTX_TOK // DEC_SEQ - 1), 0)


def _cond_row(i):
    return _cond_of_row(i * TM)


def _sigmoid(x):
    return jax.nn.sigmoid(x)


def _silu(x):
    return x * jax.nn.sigmoid(x)


def _bdot(a, w):
    return jnp.dot(a, w.astype(BF16), preferred_element_type=F32)


def _bdot_t(a, wt):
    return lax.dot_general(a, wt.astype(BF16), (((1,), (1,)), ((), ())), preferred_element_type=F32)


def _seq_pos(shape, is_ctx):
    seq_len = jnp.where(is_ctx, SEQ, DEC_SEQ)
    row = lax.broadcasted_iota(jnp.int32, shape, 0)
    return row & (seq_len - 1), seq_len


def _conv3_rows(u, w, pos, seq_len):
    n = u.shape[0]
    prev = jnp.where(pos == 0, 0.0, pltpu.roll(u, 1, 0))
    nxt = jnp.where(pos == seq_len - 1, 0.0, pltpu.roll(u, n - 1, 0))
    return w[0:1, :] * prev + w[1:2, :] * u + w[2:3, :] * nxt


def _mod_kernel(c_ref, w_ref, b_ref, o_ref):
    s = _silu(c_ref[...]).astype(BF16)
    o_ref[...] = _bdot(s, w_ref[...]) + b_ref[...]


def _modulation(cond, w_mod, b_mod):
    tn = 1024
    n = 6 * D_MODEL
    return pl.pallas_call(
        _mod_kernel,
        grid=(DEPTH, n // tn),
        in_specs=[pl.BlockSpec((COND_ROWS, D_MODEL), lambda l, j: (0, 0)),
                  pl.BlockSpec((None, D_MODEL, tn), lambda l, j: (l, 0, j)),
                  pl.BlockSpec((None, 1, tn), lambda l, j: (l, 0, j))],
        out_specs=pl.BlockSpec((None, COND_ROWS, tn), lambda l, j: (l, 0, j)),
        out_shape=jax.ShapeDtypeStruct((DEPTH, COND_ROWS, n), F32),
        compiler_params=_params(2),
        name="modulation",
    )(cond, w_mod, b_mod.reshape(DEPTH, 1, n))


def _rms(x):
    return x * lax.rsqrt(jnp.mean(x * x, axis=-1, keepdims=True) + EPS)


def _norm_mod_kernel(x_ref, g_ref, sc_ref, sh_ref, o_ref):
    y = _rms(x_ref[...]) * g_ref[...]
    o_ref[...] = (y * (1.0 + sc_ref[...]) + sh_ref[...]).astype(o_ref.dtype)


def _norm_mod(x, g, sc, sh):
    tr = 512
    per = TM // tr
    cond = lambda i: (_cond_row(i // per), 0, 0)
    return pl.pallas_call(
        _norm_mod_kernel,
        grid=(TOK // tr,),
        in_specs=[pl.BlockSpec((tr, D_MODEL), lambda i: (i, 0)),
                  pl.BlockSpec((1, D_MODEL), lambda i: (0, 0)),
                  pl.BlockSpec((None, 1, D_MODEL), cond),
                  pl.BlockSpec((None, 1, D_MODEL), cond)],
        out_specs=pl.BlockSpec((tr, D_MODEL), lambda i: (i, 0)),
        out_shape=jax.ShapeDtypeStruct((TOK, D_MODEL), BF16),
        compiler_params=_params(1),
        name="norm_mod",
    )(x, g, sc, sh)


def _final_norm_kernel(x_ref, g_ref, o_ref):
    o_ref[...] = _rms(x_ref[...]) * g_ref[...]


def _final_norm(x, g, row0, rows):
    tr = 512
    return pl.pallas_call(
        _final_norm_kernel,
        grid=(rows // tr,),
        in_specs=[pl.BlockSpec((tr, D_MODEL), lambda i: (row0 // tr + i, 0)),
                  pl.BlockSpec((1, D_MODEL), lambda i: (0, 0))],
        out_specs=pl.BlockSpec((tr, D_MODEL), lambda i: (i, 0)),
        out_shape=jax.ShapeDtypeStruct((rows, D_MODEL), F32),
        compiler_params=_params(1),
        name="final_norm",
    )(x, g)


def _mm_residual_kernel(a_ref, w_ref, x_ref, gt_lo_ref, gt_hi_ref, o_ref):
    acc = _bdot(a_ref[...], w_ref[...])
    half = acc.shape[0] // 2
    o_ref[:half, :] = x_ref[:half, :] + gt_lo_ref[...] * acc[:half]
    o_ref[half:, :] = x_ref[half:, :] + gt_hi_ref[...] * acc[half:]


def _matmul_residual(a, w, l, x, gt, tn, tm, name):
    m, k = a.shape
    n = w.shape[2]
    return pl.pallas_call(
        _mm_residual_kernel,
        grid=(m // tm, n // tn),
        in_specs=[pl.BlockSpec((tm, k), lambda i, j: (i, 0)),
                  pl.BlockSpec((None, k, tn), lambda i, j: (l, 0, j)),
                  pl.BlockSpec((tm, tn), lambda i, j: (i, j)),
                  pl.BlockSpec((None, 1, tn), lambda i, j: (_cond_of_row(i * tm), 0, j)),
                  pl.BlockSpec((None, 1, tn), lambda i, j: (_cond_of_row(i * tm + tm // 2), 0, j))],
        out_specs=pl.BlockSpec((tm, tn), lambda i, j: (i, j)),
        out_shape=jax.ShapeDtypeStruct((m, n), F32),
        compiler_params=_params(2),
        name=name,
    )(a, w, x, gt, gt)


def _tail_kernel(h_ref, wt_ref, o_ref):
    o_ref[...] = _bdot_t(h_ref[...], wt_ref[...]).astype(o_ref.dtype)


def _tail_proj(h, w_in_t, l):
    tn = 512
    in_w = w_in_t.shape[1]
    return pl.pallas_call(
        _tail_kernel,
        grid=(TOK // TMW, TAIL_W // tn),
        in_specs=[pl.BlockSpec((TMW, D_MODEL), lambda i, j: (i, 0)),
                  pl.BlockSpec((pl.Element(tn), pl.Element(D_MODEL)),
                               lambda i, j: (pl.multiple_of(l * in_w + IN_GLU + j * tn, SUBLANES), 0))],
        out_specs=pl.BlockSpec((TMW, tn), lambda i, j: (i, j)),
        out_shape=jax.ShapeDtypeStruct((TOK, TAIL_W), BF16),
        compiler_params=_params(2),
        name="tail_proj",
    )(h, w_in_t.reshape(DEPTH * in_w, D_MODEL))


def _mixer_a_kernel(h_ref, wh_ref, wb_ref, wc_ref, cw_ref, o_ref):
    is_ctx = pl.program_id(0) < CTX_TOK // TMW
    h = h_ref[...]
    p = _bdot_t(h, wc_ref[...]) * _bdot_t(h, wh_ref[...])
    pos, seq_len = _seq_pos(p.shape, is_ctx)
    o_ref[...] = (_bdot_t(h, wb_ref[...]) * _conv3_rows(p, cw_ref[...], pos, seq_len)).astype(o_ref.dtype)


def _mixer_a(h, w_in, conv_a, l):
    tn = 256
    col = lambda off: (lambda i, j: (l, off // tn + j, 0))
    wspec = lambda off: pl.BlockSpec((None, tn, D_MODEL), col(off))
    return pl.pallas_call(
        _mixer_a_kernel,
        grid=(TOK // TMW, A_W // tn),
        in_specs=[pl.BlockSpec((TMW, D_MODEL), lambda i, j: (i, 0)),
                  wspec(IN_AH), wspec(IN_AB), wspec(IN_AC),
                  pl.BlockSpec((None, A_KERNEL, tn), lambda i, j: (l, 0, j))],
        out_specs=pl.BlockSpec((TMW, tn), lambda i, j: (i, j)),
        out_shape=jax.ShapeDtypeStruct((TOK, A_W), BF16),
        compiler_params=_params(2),
        name="mixer_a",
    )(h, w_in, w_in, w_in, conv_a)


def _q_kernel(h_ref, wa_ref, g_ref, w_ref, o_ref, qn_ref):
    @pl.when(pl.program_id(1) == 0)
    def _():
        qn_ref[...] = (_rms(_bdot_t(h_ref[...], wa_ref[...])) * g_ref[...]).astype(BF16)

    o_ref[...] = _bdot(qn_ref[...], w_ref[...]).astype(o_ref.dtype)


def _q_proj(h, w_in, g_q, w_q, l):
    tn = 1024
    n = w_q.shape[2]
    return pl.pallas_call(
        _q_kernel,
        grid=(TOK // TMW, n // tn),
        in_specs=[pl.BlockSpec((TMW, D_MODEL), lambda i, j: (i, 0)),
                  pl.BlockSpec((None, Q_RANK, D_MODEL), lambda i, j: (l, IN_Q // Q_RANK, 0),
                               pipeline_mode=pl.Buffered(1)),
                  pl.BlockSpec((1, Q_RANK), lambda i, j: (0, 0)),
                  pl.BlockSpec((None, Q_RANK, tn), lambda i, j: (l, 0, j))],
        out_specs=pl.BlockSpec((TMW, tn), lambda i, j: (i, j)),
        out_shape=jax.ShapeDtypeStruct((TOK, n), BF16),
        scratch_shapes=[pltpu.VMEM((TMW, Q_RANK), BF16)],
        compiler_params=_params(2),
        name="q_proj",
    )(h, w_in, g_q, w_q)


def _kv_kernel(h_ref, wa_ref, g_ref, w_ref, ckv_ref, kpe_ref, kv_ref, cn_ref):
    @pl.when(pl.program_id(1) == 0)
    def _():
        kv_a = _bdot_t(h_ref[...], wa_ref[...])
        y = _rms(kv_a[:, :KV_RANK]) * g_ref[...]
        ckv_ref[...] = y
        kpe_ref[...] = kv_a[:, KV_RANK:]
        cn_ref[...] = y.astype(BF16)

    kv_ref[...] = _bdot(cn_ref[...], w_ref[...]).astype(kv_ref.dtype)


def _kv_proj(h, w_in, g_kv, w_ukv, l):
    tn = 512
    n = w_ukv.shape[2]
    return pl.pallas_call(
        _kv_kernel,
        grid=(TOK // TMW, n // tn),
        in_specs=[pl.BlockSpec((TMW, D_MODEL), lambda i, j: (i, 0)),
                  pl.BlockSpec((None, KV_A_W, D_MODEL), lambda i, j: (l, IN_KV // KV_A_W, 0),
                               pipeline_mode=pl.Buffered(1)),
                  pl.BlockSpec((1, KV_RANK), lambda i, j: (0, 0)),
                  pl.BlockSpec((None, KV_RANK, tn), lambda i, j: (l, 0, j))],
        out_specs=[pl.BlockSpec((TMW, KV_RANK), lambda i, j: (i, 0)),
                   pl.BlockSpec((TMW, LANES), lambda i, j: (i, 0)),
                   pl.BlockSpec((TMW, tn), lambda i, j: (i, j))],
        out_shape=[jax.ShapeDtypeStruct((TOK, KV_RANK), F32),
                   jax.ShapeDtypeStruct((TOK, LANES), F32),
                   jax.ShapeDtypeStruct((TOK, n), BF16)],
        scratch_shapes=[pltpu.VMEM((TMW, KV_RANK), BF16)],
        compiler_params=_params(2),
        name="kv_proj",
    )(h, w_in, g_kv, w_ukv)


def _cache_kv_kernel(a_ref, w_ref, o_ref):
    o_ref[...] = _bdot(a_ref[...].astype(BF16), w_ref[...]).astype(o_ref.dtype)


def _cache_kv_proj(cache_ckv, w_ukv, l):
    tn = 1024
    n = w_ukv.shape[2]
    return pl.pallas_call(
        _cache_kv_kernel,
        grid=(DEC_BATCH, n // tn),
        in_specs=[pl.BlockSpec((None, None, PAST_LEN, KV_RANK), lambda b, j: (b, l, 0, 0)),
                  pl.BlockSpec((None, KV_RANK, tn), lambda b, j: (l, 0, j))],
        out_specs=pl.BlockSpec((PAST_LEN, tn), lambda b, j: (b, j)),
        out_shape=jax.ShapeDtypeStruct((DEC_BATCH * PAST_LEN, n), BF16),
        compiler_params=_params(2),
        name="cache_kv_proj",
    )(cache_ckv, w_ukv)


def _merge_kernel(a1_ref, w1_ref, a2_ref, w2_ref, a3_ref, w3_ref, g1_ref, g2_ref, g3_ref, o_ref):
    def branch(a_ref, w_ref, g_ref):
        return _sigmoid(g_ref[...].astype(F32)) * _bdot(a_ref[...], w_ref[...])

    o_ref[...] = (branch(a1_ref, w1_ref, g1_ref) + branch(a2_ref, w2_ref, g2_ref)
                  + branch(a3_ref, w3_ref, g3_ref)).astype(o_ref.dtype)


def _merge(ya, w_a_out, attn, w_mla_out, yc, w_c_out, tail, l):
    tn = 256
    gate = lambda k: (lambda i, j: (i, (TAIL_GATE + k * D_MODEL) // tn + j))
    act = lambda width: pl.BlockSpec((TM, width), lambda i, j: (i, 0))
    wgt = lambda width: pl.BlockSpec((None, width, tn), lambda i, j: (l, 0, j))
    return pl.pallas_call(
        _merge_kernel,
        grid=(N_ROW_BLOCKS, D_MODEL // tn),
        in_specs=[act(A_W), wgt(A_W), act(N_HEADS * V_DIM), wgt(N_HEADS * V_DIM), act(C_W), wgt(C_W),
                  pl.BlockSpec((TM, tn), gate(0)), pl.BlockSpec((TM, tn), gate(1)),
                  pl.BlockSpec((TM, tn), gate(2))],
        out_specs=pl.BlockSpec((TM, tn), lambda i, j: (i, j)),
        out_shape=jax.ShapeDtypeStruct((TOK, D_MODEL), BF16),
        compiler_params=_params(2),
        name="merge",
    )(ya, w_a_out, attn, w_mla_out, yc, w_c_out, tail, tail, tail)


def _ffn_up_kernel(h_ref, wa_ref, wg_ref, ca_ref, cg_ref, o_ref):
    is_ctx = pl.program_id(0) < CTX_BLOCKS
    h = h_ref[...]
    ua = _bdot(h, wa_ref[...])
    ug = _bdot(h, wg_ref[...])
    pos, seq_len = _seq_pos(ua.shape, is_ctx)
    a = _conv3_rows(ua, ca_ref[...], pos, seq_len)
    g = _conv3_rows(ug, cg_ref[...], pos, seq_len)
    o_ref[...] = (_silu(a) * g).astype(o_ref.dtype)


def _ffn_up(h, w_up, conv_f, l):
    tn = 512
    half = D_FF // tn
    return pl.pallas_call(
        _ffn_up_kernel,
        grid=(N_ROW_BLOCKS, half),
        in_specs=[pl.BlockSpec((TM, D_MODEL), lambda i, j: (i, 0)),
                  pl.BlockSpec((None, D_MODEL, tn), lambda i, j: (l, 0, j)),
                  pl.BlockSpec((None, D_MODEL, tn), lambda i, j: (l, 0, half + j)),
                  pl.BlockSpec((None, F_KERNEL, tn), lambda i, j: (l, 0, j)),
                  pl.BlockSpec((None, F_KERNEL, tn), lambda i, j: (l, 0, half + j))],
        out_specs=pl.BlockSpec((TM, tn), lambda i, j: (i, j)),
        out_shape=jax.ShapeDtypeStruct((TOK, D_FF), BF16),
        compiler_params=_params(2),
        name="ffn_up",
    )(h, w_up, w_up, conv_f, conv_f)


def _conv_c_kernel(g1_ref, g2_ref, w_ref, lng_ref, lnb_ref, o_ref, u_ref, sh_ref, cv_ref):
    is_dec = pl.program_id(0) >= CTX_BLOCKS
    halo_w = SEQ + 2 * CONV_PAD
    n_pieces = TM // SEQ

    def gated(r0, n):
        return g1_ref[r0:r0 + n, :].astype(F32) * _sigmoid(g2_ref[r0:r0 + n, :].astype(F32))

    zeros = jnp.zeros((CONV_PAD, C_W), F32)
    u_ref[halo_w:halo_w + SUBLANES, :] = zeros[:SUBLANES]

    for p in range(n_pieces):
        r0 = p * SEQ
        top = jnp.where(is_dec, gated(r0 - CONV_PAD, CONV_PAD), 0.0) if p > 0 else zeros
        bot = jnp.where(is_dec, gated(r0 + SEQ, CONV_PAD), 0.0) if p < n_pieces - 1 else zeros
        u_ref[0:CONV_PAD, :] = top
        u_ref[CONV_PAD:CONV_PAD + SEQ, :] = gated(r0, SEQ)
        u_ref[CONV_PAD + SEQ:halo_w, :] = bot

        def shift_rows(c, carry):
            base = pl.multiple_of(c * SUBLANES, SUBLANES)
            win = u_ref[pl.ds(base, 2 * SUBLANES), :]
            sh_ref[0, pl.ds(base, SUBLANES), :] = win[:SUBLANES]
            for b in range(1, SUBLANES):
                sh_ref[b, pl.ds(base, SUBLANES), :] = pltpu.roll(win, 2 * SUBLANES - b, 0)[:SUBLANES]
            return carry

        lax.fori_loop(0, halo_w // SUBLANES, shift_rows, 0)

        for s in range(C_W // LANES):
            cols = slice(s * LANES, (s + 1) * LANES)
            taps = [jnp.broadcast_to(w_ref[k:k + 1, cols], (SUBLANES, LANES)) for k in range(C_KERNEL)]

            def conv_rows(c, carry, cols=cols, taps=taps):
                base = pl.multiple_of(c * CONV_ROWS, CONV_ROWS)
                for r in range(0, CONV_ROWS, SUBLANES):
                    acc = None
                    for k in range(C_KERNEL):
                        off = CONV_PAD - C_KERNEL // 2 + k
                        start = pl.multiple_of(base + r + (off // SUBLANES) * SUBLANES, SUBLANES)
                        term = taps[k] * sh_ref[off % SUBLANES, pl.ds(start, SUBLANES), cols]
                        acc = term if acc is None else acc + term
                    cv_ref[pl.ds(base + r, SUBLANES), cols] = acc
                return carry

            lax.fori_loop(0, SEQ // CONV_ROWS, conv_rows, 0)

        for base in range(0, SEQ, NORM_ROWS):
            v = cv_ref[base:base + NORM_ROWS, :]
            d = v - jnp.mean(v, axis=-1, keepdims=True)
            var = jnp.mean(d * d, axis=-1, keepdims=True)
            y = d * lax.rsqrt(var + EPS) * lng_ref[...] + lnb_ref[...]
            o_ref[r0 + base:r0 + base + NORM_ROWS, :] = _silu(y).astype(o_ref.dtype)


def _conv_c(tail, conv_c, ln_g, ln_b, l):
    halo_w = SEQ + 2 * CONV_PAD
    return pl.pallas_call(
        _conv_c_kernel,
        grid=(N_ROW_BLOCKS,),
        in_specs=[pl.BlockSpec((TM, C_W), lambda i: (i, TAIL_G1 // C_W)),
                  pl.BlockSpec((TM, C_W), lambda i: (i, TAIL_G2 // C_W)),
                  pl.BlockSpec((None, C_KERNEL, C_W), lambda i: (l, 0, 0)),
                  pl.BlockSpec((1, C_W), lambda i: (0, 0)),
                  pl.BlockSpec((1, C_W), lambda i: (0, 0))],
        out_specs=pl.BlockSpec((TM, C_W), lambda i: (i, 0)),
        out_shape=jax.ShapeDtypeStruct((TOK, C_W), BF16),
        scratch_shapes=[pltpu.VMEM((halo_w + SUBLANES, C_W), F32),
                        pltpu.VMEM((SUBLANES, halo_w, C_W), F32),
                        pltpu.VMEM((SEQ, C_W), F32)],
        compiler_params=_params(1),
        name="conv_c",
    )(tail, tail, conv_c, ln_g, ln_b)


def _rope_pairs(x, cos, sin):
    half = ROPE_DIM // 4
    lane = lax.broadcasted_iota(jnp.int32, x.shape, 1)
    swapped = jnp.where((lane & (2 * half - 1)) < half,
                        pltpu.roll(x, LANES - half, 1), pltpu.roll(x, half, 1))
    return x * cos + swapped * sin


def _pe_keys(k):
    lane = lax.broadcasted_iota(jnp.int32, k.shape, 1)
    lo = jnp.where(lane < ROPE_DIM, k, 0.0)
    return lo.astype(BF16), pltpu.roll(lo, ROPE_DIM, 1).astype(BF16)


def _attend(q_ref, segments, o_ref, rope_q=None):
    c = QK_DIM ** -0.5 * math.log2(math.e)
    pe_col0 = N_HEADS * NOPE_DIM
    for hp in range(HEAD_PAIRS):
        qp = q_ref[:, pe_col0 + hp * LANES:pe_col0 + (hp + 1) * LANES]
        if rope_q is not None:
            qp = _rope_pairs(qp.astype(F32), *rope_q).astype(BF16)
        for h in range(2):
            head = 2 * hp + h
            q = jnp.concatenate([q_ref[:, head * NOPE_DIM:(head + 1) * NOPE_DIM], qp], axis=1)
            kbase = head * (NOPE_DIM + V_DIM)
            scores = []
            for kv_ref, pe in segments:
                k = jnp.concatenate([kv_ref[:, kbase:kbase + NOPE_DIM], pe[h]], axis=1)
                scores.append(lax.dot_general(q, k, (((1,), (1,)), ((), ())),
                                              preferred_element_type=F32))
            m = scores[0].max(axis=-1, keepdims=True)
            for s in scores[1:]:
                m = jnp.maximum(m, s.max(axis=-1, keepdims=True))
            denom = 0.0
            out = 0.0
            for s, (kv_ref, _) in zip(scores, segments):
                p = jnp.exp2((s - m) * c)
                denom = denom + p.sum(axis=-1, keepdims=True)
                out = out + jnp.dot(p.astype(BF16), kv_ref[:, kbase + NOPE_DIM:kbase + NOPE_DIM + V_DIM],
                                    preferred_element_type=F32)
            o_ref[:, head * V_DIM:(head + 1) * V_DIM] = (out / denom).astype(o_ref.dtype)


def _attn_kernel(q_ref, kvx_ref, kpex_ref, kvd_ref, kped_ref, kvc_ref, kpec_ref,
                 cq_ref, sq_ref, ck_ref, sk_ref, o_ref):
    step = pl.program_id(0)

    @pl.when(step < BATCH)
    def _():
        _attend(q_ref, [(kvx_ref, _pe_keys(kpex_ref[...]))], o_ref)

    @pl.when(step >= BATCH)
    def _():
        new_pe = _rope_pairs(kped_ref[...], ck_ref[...], sk_ref[...])
        _attend(q_ref, [(kvc_ref, _pe_keys(kpec_ref[...])), (kvd_ref, _pe_keys(new_pe))], o_ref,
                rope_q=(cq_ref[...], sq_ref[...]))


def _attention(q, kv, kpe, kv_cache, kpe_cache, cos, sin):
    per_seq = DEC_SEQ // TQ
    n_steps = BATCH + DEC_BATCH * per_seq
    ctx = lambda s: jnp.minimum(s, BATCH - 1)
    dec = lambda s: jnp.maximum(s - BATCH, 0)
    kv_w = N_HEADS * (NOPE_DIM + V_DIM)
    in_specs = [
        pl.BlockSpec((TQ, q.shape[1]), lambda s: (s, 0)),
        pl.BlockSpec((SEQ, kv_w), lambda s: (ctx(s), 0)),
        pl.BlockSpec((SEQ, LANES), lambda s: (ctx(s), 0)),
        pl.BlockSpec((DEC_SEQ, kv_w), lambda s: (CTX_TOK // DEC_SEQ + dec(s) // per_seq, 0)),
        pl.BlockSpec((DEC_SEQ, LANES), lambda s: (CTX_TOK // DEC_SEQ + dec(s) // per_seq, 0)),
        pl.BlockSpec((PAST_LEN, kv_w), lambda s: (dec(s) // per_seq, 0)),
        pl.BlockSpec((None, PAST_LEN, LANES), lambda s: (dec(s) // per_seq, 0, 0)),
        pl.BlockSpec((TQ, LANES), lambda s: (dec(s) % per_seq, 0)),
        pl.BlockSpec((TQ, LANES), lambda s: (dec(s) % per_seq, 0)),
        pl.BlockSpec((DEC_SEQ, LANES), lambda s: (0, 0)),
        pl.BlockSpec((DEC_SEQ, LANES), lambda s: (0, 0)),
    ]
    return pl.pallas_call(
        _attn_kernel,
        grid=(n_steps,),
        in_specs=in_specs,
        out_specs=pl.BlockSpec((TQ, N_HEADS * V_DIM), lambda s: (s, 0)),
        out_shape=jax.ShapeDtypeStruct((TOK, N_HEADS * V_DIM), BF16),
        compiler_params=_params(1),
        name="attention",
    )(q, kv, kpe, kv, kpe, kv_cache, kpe_cache, cos, sin, cos, sin)


def _rope_tables():
    rows = DEC_SEQ // GRID_W
    r = jnp.repeat(jnp.arange(rows), GRID_W).astype(F32)
    col = jnp.tile(jnp.arange(GRID_W), rows).astype(F32)
    half_axis = ROPE_DIM // 2
    inv = ROPE_THETA ** (-jnp.arange(0, half_axis, 2, dtype=F32) / half_axis)
    ar = r[:, None] * inv
    ac = col[:, None] * inv
    cos = jnp.concatenate([jnp.cos(ar), jnp.cos(ar), jnp.cos(ac), jnp.cos(ac)], axis=-1)
    sin = jnp.concatenate([-jnp.sin(ar), jnp.sin(ar), -jnp.sin(ac), jnp.sin(ac)], axis=-1)
    reps = LANES // ROPE_DIM
    return jnp.tile(cos, (1, reps)), jnp.tile(sin, (1, reps))


def kernel(x_prompt, x_sample, cache_ckv, cache_kpe, c, c_ctx, w_mod, b_mod, g_norm1, g_norm2, w_in, conv_a, w_a_out, g_q, w_uq, g_kv, w_ukv, w_mla_out, conv_c, ln_c_g, ln_c_b, w_c_out, w_o, w_up, conv_f, w_down, g_final):
    x = jnp.concatenate([x_prompt.reshape(CTX_TOK, D_MODEL), x_sample.reshape(DEC_TOK, D_MODEL)], axis=0)
    cond = jnp.concatenate([c_ctx[None, :], c, jnp.zeros((COND_ROWS - N_COND, D_MODEL), F32)], axis=0)
    mod = _modulation(cond, w_mod, b_mod)
    cos, sin = _rope_tables()

    w_in_t = jnp.swapaxes(w_in, 1, 2)
    wq = w_uq.reshape(DEPTH, Q_RANK, N_HEADS, QK_DIM)
    w_q = jnp.concatenate([wq[..., :NOPE_DIM].reshape(DEPTH, Q_RANK, N_HEADS * NOPE_DIM),
                           wq[..., NOPE_DIM:].reshape(DEPTH, Q_RANK, N_HEADS * ROPE_DIM)],
                          axis=2).astype(BF16)
    kpe_cache = jnp.pad(cache_kpe, ((0, 0), (0, 0), (0, 0), (0, LANES - ROPE_DIM)))

    ckv_layers, kpe_layers = [], []
    for l in range(DEPTH):
        sh1, sc1, gt1, sh2, sc2, gt2 = [
            mod[l, :N_COND, k * D_MODEL:(k + 1) * D_MODEL].reshape(N_COND, 1, D_MODEL) for k in range(6)]

        h = _norm_mod(x, g_norm1[l][None, :], sc1, sh1)
        ya = _mixer_a(h, w_in_t, conv_a, l)
        q = _q_proj(h, w_in_t, g_q[l][None, :], w_q, l)
        ckv, kpe, kv = _kv_proj(h, w_in_t, g_kv[l][None, :], w_ukv, l)
        tail = _tail_proj(h, w_in_t, l)
        kv_cache = _cache_kv_proj(cache_ckv, w_ukv, l)
        attn = _attention(q, kv, kpe, kv_cache, kpe_cache[:, l], cos, sin)
        yc = _conv_c(tail, conv_c, ln_c_g[l][None, :], ln_c_b[l][None, :], l)
        merged = _merge(ya, w_a_out, attn, w_mla_out, yc, w_c_out, tail, l)
        x = _matmul_residual(merged, w_o, l, x, gt1, tn=512, tm=TMW, name="out_proj")

        h = _norm_mod(x, g_norm2[l][None, :], sc2, sh2)
        act = _ffn_up(h, w_up, conv_f, l)
        x = _matmul_residual(act, w_down, l, x, gt2, tn=256, tm=TM, name="ffn_down")

        ckv_layers.append(ckv[:CTX_TOK].reshape(BATCH, SEQ, KV_RANK))
        kpe_layers.append(kpe[:CTX_TOK, :ROPE_DIM].reshape(BATCH, SEQ, ROPE_DIM))

    g = g_final[None, :]
    y_prompt = _final_norm(x, g, 0, CTX_TOK).reshape(BATCH, SEQ, D_MODEL)
    y_sample = _final_norm(x, g, CTX_TOK, DEC_TOK).reshape(DEC_BATCH, DEC_SEQ, D_MODEL)
    return (y_prompt, y_sample, jnp.stack(ckv_layers, axis=1), jnp.stack(kpe_layers, axis=1))
```

```python
import math

import jax
import jax.numpy as jnp
from jax import lax
from jax.experimental import pallas as pl
from jax.experimental.pallas import tpu as pltpu

D_MODEL = 2048
BATCH = 16
SEQ = 256
DEPTH = 2
DEC_BATCH = 2
DEC_SEQ = 1024
PAST_LEN = 256
GRID_W = 64
A_W = 1024
A_KERNEL = 3
N_HEADS = 16
Q_RANK = 768
KV_RANK = 512
NOPE_DIM = 128
ROPE_DIM = 64
V_DIM = 128
QK_DIM = NOPE_DIM + ROPE_DIM
C_W = 1024
C_KERNEL = 31
D_FF = 5632
F_KERNEL = 3
ROPE_THETA = 10000.0
EPS = 1e-6

CTX_TOK = BATCH * SEQ
DEC_TOK = DEC_BATCH * DEC_SEQ
TOK = CTX_TOK + DEC_TOK
TM = 1024
N_ROW_BLOCKS = TOK // TM
CTX_BLOCKS = CTX_TOK // TM
TMW = 2048
N_COND = 3
COND_ROWS = 8
LANES = 128
SUBLANES = 8
HEAD_PAIRS = N_HEADS // 2
KV_A_W = 640
CONV_PAD = 16
CONV_ROWS = 64
NORM_ROWS = 64
TQ = 256
VMEM_LIMIT = 56 * 1024 * 1024

IN_AH, IN_AB, IN_AC = 0, A_W, 2 * A_W
IN_Q = 3 * A_W
IN_KV = IN_Q + Q_RANK
IN_GLU = IN_KV + KV_RANK + ROPE_DIM
TAIL_G1, TAIL_G2, TAIL_GATE = 0, C_W, 2 * C_W
TAIL_W = 2 * C_W + 3 * D_MODEL

F32 = jnp.float32
BF16 = jnp.bfloat16


def _params(n_axes):
    return pltpu.CompilerParams(dimension_semantics=("arbitrary",) * n_axes,
                                vmem_limit_bytes=VMEM_LIMIT)


def _cond_of_row(row):
    return jnp.maximum(row // DEC_SEQ - (CTX_TOK // DEC_SEQ - 1), 0)


def _cond_row(i):
    return _cond_of_row(i * TM)


def _sigmoid(x):
    return jax.nn.sigmoid(x)


def _silu(x):
    return x * jax.nn.sigmoid(x)


def _bdot(a, w):
    return jnp.dot(a, w.astype(BF16), preferred_element_type=F32)


def _bdot_t(a, wt):
    return lax.dot_general(a, wt.astype(BF16), (((1,), (1,)), ((), ())), preferred_element_type=F32)


def _seq_pos(shape, is_ctx):
    seq_len = jnp.where(is_ctx, SEQ, DEC_SEQ)
    row = lax.broadcasted_iota(jnp.int32, shape, 0)
    return row & (seq_len - 1), seq_len


def _conv3_rows(u, w, pos, seq_len):
    n = u.shape[0]
    prev = jnp.where(pos == 0, 0.0, pltpu.roll(u, 1, 0))
    nxt = jnp.where(pos == seq_len - 1, 0.0, pltpu.roll(u, n - 1, 0))
    return w[0:1, :] * prev + w[1:2, :] * u + w[2:3, :] * nxt


def _mod_kernel(c_ref, w_ref, b_ref, o_ref):
    s = _silu(c_ref[...]).astype(BF16)
    o_ref[...] = _bdot(s, w_ref[...]) + b_ref[...]


def _modulation(cond, w_mod, b_mod):
    tn = 1024
    n = 6 * D_MODEL
    return pl.pallas_call(
        _mod_kernel,
        grid=(DEPTH, n // tn),
        in_specs=[pl.BlockSpec((COND_ROWS, D_MODEL), lambda l, j: (0, 0)),
                  pl.BlockSpec((None, D_MODEL, tn), lambda l, j: (l, 0, j)),
                  pl.BlockSpec((None, 1, tn), lambda l, j: (l, 0, j))],
        out_specs=pl.BlockSpec((None, COND_ROWS, tn), lambda l, j: (l, 0, j)),
        out_shape=jax.ShapeDtypeStruct((DEPTH, COND_ROWS, n), F32),
        compiler_params=_params(2),
        name="modulation",
    )(cond, w_mod, b_mod.reshape(DEPTH, 1, n))


def _rms(x):
    return x * lax.rsqrt(jnp.mean(x * x, axis=-1, keepdims=True) + EPS)


def _norm_mod_kernel(x_ref, g_ref, sc_ref, sh_ref, o_ref):
    y = _rms(x_ref[...]) * g_ref[...]
    o_ref[...] = (y * (1.0 + sc_ref[...]) + sh_ref[...]).astype(o_ref.dtype)


def _norm_mod(x, g, sc, sh):
    tr = 512
    per = TM // tr
    cond = lambda i: (_cond_row(i // per), 0, 0)
    return pl.pallas_call(
        _norm_mod_kernel,
        grid=(TOK // tr,),
        in_specs=[pl.BlockSpec((tr, D_MODEL), lambda i: (i, 0)),
                  pl.BlockSpec((1, D_MODEL), lambda i: (0, 0)),
                  pl.BlockSpec((None, 1, D_MODEL), cond),
                  pl.BlockSpec((None, 1, D_MODEL), cond)],
        out_specs=pl.BlockSpec((tr, D_MODEL), lambda i: (i, 0)),
        out_shape=jax.ShapeDtypeStruct((TOK, D_MODEL), BF16),
        compiler_params=_params(1),
        name="norm_mod",
    )(x, g, sc, sh)


def _final_norm_kernel(x_ref, g_ref, o_ref):
    o_ref[...] = _rms(x_ref[...]) * g_ref[...]


def _final_norm(x, g, row0, rows):
    tr = 512
    return pl.pallas_call(
        _final_norm_kernel,
        grid=(rows // tr,),
        in_specs=[pl.BlockSpec((tr, D_MODEL), lambda i: (row0 // tr + i, 0)),
                  pl.BlockSpec((1, D_MODEL), lambda i: (0, 0))],
        out_specs=pl.BlockSpec((tr, D_MODEL), lambda i: (i, 0)),
        out_shape=jax.ShapeDtypeStruct((rows, D_MODEL), F32),
        compiler_params=_params(1),
        name="final_norm",
    )(x, g)


def _mm_residual_kernel(a_ref, w_ref, x_ref, gt_lo_ref, gt_hi_ref, o_ref):
    acc = _bdot(a_ref[...], w_ref[...])
    half = acc.shape[0] // 2
    o_ref[:half, :] = x_ref[:half, :] + gt_lo_ref[...] * acc[:half]
    o_ref[half:, :] = x_ref[half:, :] + gt_hi_ref[...] * acc[half:]


def _matmul_residual(a, w, l, x, gt, tn, tm, a_buffers, name):
    m, k = a.shape
    n = w.shape[2]
    return pl.pallas_call(
        _mm_residual_kernel,
        grid=(m // tm, n // tn),
        in_specs=[pl.BlockSpec((tm, k), lambda i, j: (i, 0), pipeline_mode=pl.Buffered(a_buffers)),
                  pl.BlockSpec((None, k, tn), lambda i, j: (l, 0, j)),
                  pl.BlockSpec((tm, tn), lambda i, j: (i, j)),
                  pl.BlockSpec((None, 1, tn), lambda i, j: (_cond_of_row(i * tm), 0, j)),
                  pl.BlockSpec((None, 1, tn), lambda i, j: (_cond_of_row(i * tm + tm // 2), 0, j))],
        out_specs=pl.BlockSpec((tm, tn), lambda i, j: (i, j)),
        out_shape=jax.ShapeDtypeStruct((m, n), F32),
        compiler_params=_params(2),
        name=name,
    )(a, w, x, gt, gt)


def _tail_kernel(h_ref, wt_ref, o_ref):
    o_ref[...] = _bdot_t(h_ref[...], wt_ref[...]).astype(o_ref.dtype)


def _tail_proj(h, w_in_t, l):
    tn = 512
    in_w = w_in_t.shape[1]
    return pl.pallas_call(
        _tail_kernel,
        grid=(TOK // TMW, TAIL_W // tn),
        in_specs=[pl.BlockSpec((TMW, D_MODEL), lambda i, j: (i, 0)),
                  pl.BlockSpec((pl.Element(tn), pl.Element(D_MODEL)),
                               lambda i, j: (pl.multiple_of(l * in_w + IN_GLU + j * tn, SUBLANES), 0))],
        out_specs=pl.BlockSpec((TMW, tn), lambda i, j: (i, j)),
        out_shape=jax.ShapeDtypeStruct((TOK, TAIL_W), BF16),
        compiler_params=_params(2),
        name="tail_proj",
    )(h, w_in_t.reshape(DEPTH * in_w, D_MODEL))


def _mixer_a_kernel(h_ref, wh_ref, wb_ref, wc_ref, cw_ref, o_ref):
    is_ctx = pl.program_id(0) < CTX_TOK // TMW
    h = h_ref[...]
    p = _bdot_t(h, wc_ref[...]) * _bdot_t(h, wh_ref[...])
    pos, seq_len = _seq_pos(p.shape, is_ctx)
    o_ref[...] = (_bdot_t(h, wb_ref[...]) * _conv3_rows(p, cw_ref[...], pos, seq_len)).astype(o_ref.dtype)


def _mixer_a(h, w_in, conv_a, l):
    tn = 256
    col = lambda off: (lambda i, j: (l, off // tn + j, 0))
    wspec = lambda off: pl.BlockSpec((None, tn, D_MODEL), col(off))
    return pl.pallas_call(
        _mixer_a_kernel,
        grid=(TOK // TMW, A_W // tn),
        in_specs=[pl.BlockSpec((TMW, D_MODEL), lambda i, j: (i, 0)),
                  wspec(IN_AH), wspec(IN_AB), wspec(IN_AC),
                  pl.BlockSpec((None, A_KERNEL, tn), lambda i, j: (l, 0, j))],
        out_specs=pl.BlockSpec((TMW, tn), lambda i, j: (i, j)),
        out_shape=jax.ShapeDtypeStruct((TOK, A_W), BF16),
        compiler_params=_params(2),
        name="mixer_a",
    )(h, w_in, w_in, w_in, conv_a)


def _q_kernel(h_ref, wa_ref, g_ref, w_ref, o_ref, qn_ref):
    @pl.when(pl.program_id(1) == 0)
    def _():
        qn_ref[...] = (_rms(_bdot_t(h_ref[...], wa_ref[...])) * g_ref[...]).astype(BF16)

    o_ref[...] = _bdot(qn_ref[...], w_ref[...]).astype(o_ref.dtype)


def _q_proj(h, w_in, g_q, w_q, l):
    tn = 1024
    n = w_q.shape[2]
    return pl.pallas_call(
        _q_kernel,
        grid=(TOK // TMW, n // tn),
        in_specs=[pl.BlockSpec((TMW, D_MODEL), lambda i, j: (i, 0)),
                  pl.BlockSpec((None, Q_RANK, D_MODEL), lambda i, j: (l, IN_Q // Q_RANK, 0),
                               pipeline_mode=pl.Buffered(1)),
                  pl.BlockSpec((1, Q_RANK), lambda i, j: (0, 0)),
                  pl.BlockSpec((None, Q_RANK, tn), lambda i, j: (l, 0, j))],
        out_specs=pl.BlockSpec((TMW, tn), lambda i, j: (i, j)),
        out_shape=jax.ShapeDtypeStruct((TOK, n), BF16),
        scratch_shapes=[pltpu.VMEM((TMW, Q_RANK), BF16)],
        compiler_params=_params(2),
        name="q_proj",
    )(h, w_in, g_q, w_q)


def _kv_kernel(h_ref, wa_ref, g_ref, w_ref, ckv_ref, kpe_ref, kv_ref, cn_ref):
    @pl.when(pl.program_id(1) == 0)
    def _():
        kv_a = _bdot_t(h_ref[...], wa_ref[...])
        y = _rms(kv_a[:, :KV_RANK]) * g_ref[...]
        ckv_ref[...] = y
        kpe_ref[...] = kv_a[:, KV_RANK:]
        cn_ref[...] = y.astype(BF16)

    kv_ref[...] = _bdot(cn_ref[...], w_ref[...]).astype(kv_ref.dtype)


def _kv_proj(h, w_in, g_kv, w_ukv, l):
    tn = 512
    n = w_ukv.shape[2]
    return pl.pallas_call(
        _kv_kernel,
        grid=(TOK // TMW, n // tn),
        in_specs=[pl.BlockSpec((TMW, D_MODEL), lambda i, j: (i, 0)),
                  pl.BlockSpec((None, KV_A_W, D_MODEL), lambda i, j: (l, IN_KV // KV_A_W, 0),
                               pipeline_mode=pl.Buffered(1)),
                  pl.BlockSpec((1, KV_RANK), lambda i, j: (0, 0)),
                  pl.BlockSpec((None, KV_RANK, tn), lambda i, j: (l, 0, j))],
        out_specs=[pl.BlockSpec((TMW, KV_RANK), lambda i, j: (i, 0)),
                   pl.BlockSpec((TMW, LANES), lambda i, j: (i, 0)),
                   pl.BlockSpec((TMW, tn), lambda i, j: (i, j))],
        out_shape=[jax.ShapeDtypeStruct((TOK, KV_RANK), F32),
                   jax.ShapeDtypeStruct((TOK, LANES), F32),
                   jax.ShapeDtypeStruct((TOK, n), BF16)],
        scratch_shapes=[pltpu.VMEM((TMW, KV_RANK), BF16)],
        compiler_params=_params(2),
        name="kv_proj",
    )(h, w_in, g_kv, w_ukv)


def _cache_kv_kernel(a_ref, w_ref, o_ref):
    o_ref[...] = _bdot(a_ref[...].astype(BF16), w_ref[...]).astype(o_ref.dtype)


def _cache_kv_proj(cache_ckv, w_ukv, l):
    tn = 1024
    n = w_ukv.shape[2]
    return pl.pallas_call(
        _cache_kv_kernel,
        grid=(DEC_BATCH, n // tn),
        in_specs=[pl.BlockSpec((None, None, PAST_LEN, KV_RANK), lambda b, j: (b, l, 0, 0)),
                  pl.BlockSpec((None, KV_RANK, tn), lambda b, j: (l, 0, j))],
        out_specs=pl.BlockSpec((PAST_LEN, tn), lambda b, j: (b, j)),
        out_shape=jax.ShapeDtypeStruct((DEC_BATCH * PAST_LEN, n), BF16),
        compiler_params=_params(2),
        name="cache_kv_proj",
    )(cache_ckv, w_ukv)


def _merge_kernel(a1_ref, w1_ref, a2_ref, w2_ref, a3_ref, w3_ref, g1_ref, g2_ref, g3_ref, o_ref):
    def branch(a_ref, w_ref, g_ref):
        return _sigmoid(g_ref[...].astype(F32)) * _bdot(a_ref[...], w_ref[...])

    o_ref[...] = (branch(a1_ref, w1_ref, g1_ref) + branch(a2_ref, w2_ref, g2_ref)
                  + branch(a3_ref, w3_ref, g3_ref)).astype(o_ref.dtype)


def _merge(ya, w_a_out, attn, w_mla_out, yc, w_c_out, tail, l):
    tn = 256
    gate = lambda k: (lambda i, j: (i, (TAIL_GATE + k * D_MODEL) // tn + j))
    act = lambda width: pl.BlockSpec((TMW, width), lambda i, j: (i, 0), pipeline_mode=pl.Buffered(1))
    wgt = lambda width: pl.BlockSpec((None, width, tn), lambda i, j: (l, 0, j))
    return pl.pallas_call(
        _merge_kernel,
        grid=(TOK // TMW, D_MODEL // tn),
        in_specs=[act(A_W), wgt(A_W), act(N_HEADS * V_DIM), wgt(N_HEADS * V_DIM), act(C_W), wgt(C_W),
                  pl.BlockSpec((TMW, tn), gate(0)), pl.BlockSpec((TMW, tn), gate(1)),
                  pl.BlockSpec((TMW, tn), gate(2))],
        out_specs=pl.BlockSpec((TMW, tn), lambda i, j: (i, j)),
        out_shape=jax.ShapeDtypeStruct((TOK, D_MODEL), BF16),
        compiler_params=_params(2),
        name="merge",
    )(ya, w_a_out, attn, w_mla_out, yc, w_c_out, tail, tail, tail)


FFN_TN = 512
FFN_COL_TILES = D_FF // FFN_TN


def _ffn_up_kernel(h_ref, wa_ref, wg_ref, ca_ref, cg_ref, o_ref):
    is_ctx = pl.program_id(0) < CTX_BLOCKS
    h = h_ref[...]
    ua = _bdot(h, wa_ref[...])
    ug = _bdot(h, wg_ref[...])
    pos, seq_len = _seq_pos(ua.shape, is_ctx)
    a = _conv3_rows(ua, ca_ref[...], pos, seq_len)
    g = _conv3_rows(ug, cg_ref[...], pos, seq_len)
    o_ref[...] = (_silu(a) * g).astype(o_ref.dtype)


def _ffn_up(h, w_up, conv_f, l):
    tn = FFN_TN
    half = FFN_COL_TILES
    return pl.pallas_call(
        _ffn_up_kernel,
        grid=(N_ROW_BLOCKS, half),
        in_specs=[pl.BlockSpec((TM, D_MODEL), lambda i, j: (i, 0)),
                  pl.BlockSpec((None, D_MODEL, tn), lambda i, j: (l, 0, j)),
                  pl.BlockSpec((None, D_MODEL, tn), lambda i, j: (l, 0, half + j)),
                  pl.BlockSpec((None, F_KERNEL, tn), lambda i, j: (l, 0, j)),
                  pl.BlockSpec((None, F_KERNEL, tn), lambda i, j: (l, 0, half + j))],
        out_specs=pl.BlockSpec((TM, tn), lambda i, j: (i, j)),
        out_shape=jax.ShapeDtypeStruct((TOK, D_FF), BF16),
        compiler_params=_params(2),
        name="ffn_up",
    )(h, w_up, w_up, conv_f, conv_f)


def _conv_c_kernel(g1_ref, g2_ref, w_ref, lng_ref, lnb_ref, o_ref, u_ref, sh_ref, cv_ref):
    is_dec = pl.program_id(0) >= CTX_BLOCKS
    halo_w = SEQ + 2 * CONV_PAD
    n_pieces = TM // SEQ

    def gated(r0, n):
        return g1_ref[r0:r0 + n, :].astype(F32) * _sigmoid(g2_ref[r0:r0 + n, :].astype(F32))

    zeros = jnp.zeros((CONV_PAD, C_W), F32)
    u_ref[halo_w:halo_w + SUBLANES, :] = zeros[:SUBLANES]

    for p in range(n_pieces):
        r0 = p * SEQ
        top = jnp.where(is_dec, gated(r0 - CONV_PAD, CONV_PAD), 0.0) if p > 0 else zeros
        bot = jnp.where(is_dec, gated(r0 + SEQ, CONV_PAD), 0.0) if p < n_pieces - 1 else zeros
        u_ref[0:CONV_PAD, :] = top
        u_ref[CONV_PAD:CONV_PAD + SEQ, :] = gated(r0, SEQ)
        u_ref[CONV_PAD + SEQ:halo_w, :] = bot

        def shift_rows(c, carry):
            base = pl.multiple_of(c * SUBLANES, SUBLANES)
            win = u_ref[pl.ds(base, 2 * SUBLANES), :]
            sh_ref[0, pl.ds(base, SUBLANES), :] = win[:SUBLANES]
            for b in range(1, SUBLANES):
                sh_ref[b, pl.ds(base, SUBLANES), :] = pltpu.roll(win, 2 * SUBLANES - b, 0)[:SUBLANES]
            return carry

        lax.fori_loop(0, halo_w // SUBLANES, shift_rows, 0)

        for s in range(C_W // LANES):
            cols = slice(s * LANES, (s + 1) * LANES)
            taps = [jnp.broadcast_to(w_ref[k:k + 1, cols], (SUBLANES, LANES)) for k in range(C_KERNEL)]

            def conv_rows(c, carry, cols=cols, taps=taps):
                base = pl.multiple_of(c * CONV_ROWS, CONV_ROWS)
                for r in range(0, CONV_ROWS, SUBLANES):
                    acc = None
                    for k in range(C_KERNEL):
                        off = CONV_PAD - C_KERNEL // 2 + k
                        start = pl.multiple_of(base + r + (off // SUBLANES) * SUBLANES, SUBLANES)
                        term = taps[k] * sh_ref[off % SUBLANES, pl.ds(start, SUBLANES), cols]
                        acc = term if acc is None else acc + term
                    cv_ref[pl.ds(base + r, SUBLANES), cols] = acc
                return carry

            lax.fori_loop(0, SEQ // CONV_ROWS, conv_rows, 0)

        for base in range(0, SEQ, NORM_ROWS):
            v = cv_ref[base:base + NORM_ROWS, :]
            d = v - jnp.mean(v, axis=-1, keepdims=True)
            var = jnp.mean(d * d, axis=-1, keepdims=True)
            y = d * lax.rsqrt(var + EPS) * lng_ref[...] + lnb_ref[...]
            o_ref[r0 + base:r0 + base + NORM_ROWS, :] = _silu(y).astype(o_ref.dtype)


def _conv_c(tail, conv_c, ln_g, ln_b, l):
    halo_w = SEQ + 2 * CONV_PAD
    return pl.pallas_call(
        _conv_c_kernel,
        grid=(N_ROW_BLOCKS,),
        in_specs=[pl.BlockSpec((TM, C_W), lambda i: (i, TAIL_G1 // C_W)),
                  pl.BlockSpec((TM, C_W), lambda i: (i, TAIL_G2 // C_W)),
                  pl.BlockSpec((None, C_KERNEL, C_W), lambda i: (l, 0, 0)),
                  pl.BlockSpec((1, C_W), lambda i: (0, 0)),
                  pl.BlockSpec((1, C_W), lambda i: (0, 0))],
        out_specs=pl.BlockSpec((TM, C_W), lambda i: (i, 0)),
        out_shape=jax.ShapeDtypeStruct((TOK, C_W), BF16),
        scratch_shapes=[pltpu.VMEM((halo_w + SUBLANES, C_W), F32),
                        pltpu.VMEM((SUBLANES, halo_w, C_W), F32),
                        pltpu.VMEM((SEQ, C_W), F32)],
        compiler_params=_params(1),
        name="conv_c",
    )(tail, tail, conv_c, ln_g, ln_b)


def _rope_pairs(x, cos, sin):
    half = ROPE_DIM // 4
    lane = lax.broadcasted_iota(jnp.int32, x.shape, 1)
    swapped = jnp.where((lane & (2 * half - 1)) < half,
                        pltpu.roll(x, LANES - half, 1), pltpu.roll(x, half, 1))
    return x * cos + swapped * sin


def _pe_keys(k):
    lane = lax.broadcasted_iota(jnp.int32, k.shape, 1)
    lo = jnp.where(lane < ROPE_DIM, k, 0.0)
    return lo.astype(BF16), pltpu.roll(lo, ROPE_DIM, 1).astype(BF16)


def _attend(q_ref, segments, o_ref, rope_q=None):
    c = QK_DIM ** -0.5 * math.log2(math.e)
    pe_col0 = N_HEADS * NOPE_DIM
    for hp in range(HEAD_PAIRS):
        qp = q_ref[:, pe_col0 + hp * LANES:pe_col0 + (hp + 1) * LANES]
        if rope_q is not None:
            qp = _rope_pairs(qp.astype(F32), *rope_q).astype(BF16)
        for h in range(2):
            head = 2 * hp + h
            q = jnp.concatenate([q_ref[:, head * NOPE_DIM:(head + 1) * NOPE_DIM], qp], axis=1)
            kbase = head * (NOPE_DIM + V_DIM)
            scores = []
            for kv_ref, pe in segments:
                k = jnp.concatenate([kv_ref[:, kbase:kbase + NOPE_DIM], pe[h]], axis=1)
                scores.append(lax.dot_general(q, k, (((1,), (1,)), ((), ())),
                                              preferred_element_type=F32))
            m = scores[0].max(axis=-1, keepdims=True)
            for s in scores[1:]:
                m = jnp.maximum(m, s.max(axis=-1, keepdims=True))
            denom = 0.0
            out = 0.0
            for s, (kv_ref, _) in zip(scores, segments):
                p = jnp.exp2((s - m) * c)
                denom = denom + p.sum(axis=-1, keepdims=True)
                out = out + jnp.dot(p.astype(BF16), kv_ref[:, kbase + NOPE_DIM:kbase + NOPE_DIM + V_DIM],
                                    preferred_element_type=F32)
            o_ref[:, head * V_DIM:(head + 1) * V_DIM] = (out / denom).astype(o_ref.dtype)


def _attn_kernel(q_ref, kvx_ref, kpex_ref, kvd_ref, kped_ref, kvc_ref, kpec_ref,
                 cq_ref, sq_ref, ck_ref, sk_ref, o_ref):
    step = pl.program_id(0)

    @pl.when(step < BATCH)
    def _():
        _attend(q_ref, [(kvx_ref, _pe_keys(kpex_ref[...]))], o_ref)

    @pl.when(step >= BATCH)
    def _():
        new_pe = _rope_pairs(kped_ref[...], ck_ref[...], sk_ref[...])
        _attend(q_ref, [(kvc_ref, _pe_keys(kpec_ref[...])), (kvd_ref, _pe_keys(new_pe))], o_ref,
                rope_q=(cq_ref[...], sq_ref[...]))


def _attention(q, kv, kpe, kv_cache, kpe_cache, cos, sin):
    per_seq = DEC_SEQ // TQ
    n_steps = BATCH + DEC_BATCH * per_seq
    ctx = lambda s: jnp.minimum(s, BATCH - 1)
    dec = lambda s: jnp.maximum(s - BATCH, 0)
    kv_w = N_HEADS * (NOPE_DIM + V_DIM)
    in_specs = [
        pl.BlockSpec((TQ, q.shape[1]), lambda s: (s, 0)),
        pl.BlockSpec((SEQ, kv_w), lambda s: (ctx(s), 0)),
        pl.BlockSpec((SEQ, LANES), lambda s: (ctx(s), 0)),
        pl.BlockSpec((DEC_SEQ, kv_w), lambda s: (CTX_TOK // DEC_SEQ + dec(s) // per_seq, 0)),
        pl.BlockSpec((DEC_SEQ, LANES), lambda s: (CTX_TOK // DEC_SEQ + dec(s) // per_seq, 0)),
        pl.BlockSpec((PAST_LEN, kv_w), lambda s: (dec(s) // per_seq, 0)),
        pl.BlockSpec((None, PAST_LEN, LANES), lambda s: (dec(s) // per_seq, 0, 0)),
        pl.BlockSpec((TQ, LANES), lambda s: (dec(s) % per_seq, 0)),
        pl.BlockSpec((TQ, LANES), lambda s: (dec(s) % per_seq, 0)),
        pl.BlockSpec((DEC_SEQ, LANES), lambda s: (0, 0)),
        pl.BlockSpec((DEC_SEQ, LANES), lambda s: (0, 0)),
    ]
    return pl.pallas_call(
        _attn_kernel,
        grid=(n_steps,),
        in_specs=in_specs,
        out_specs=pl.BlockSpec((TQ, N_HEADS * V_DIM), lambda s: (s, 0)),
        out_shape=jax.ShapeDtypeStruct((TOK, N_HEADS * V_DIM), BF16),
        compiler_params=_params(1),
        name="attention",
    )(q, kv, kpe, kv, kpe, kv_cache, kpe_cache, cos, sin, cos, sin)


def _rope_tables():
    rows = DEC_SEQ // GRID_W
    r = jnp.repeat(jnp.arange(rows), GRID_W).astype(F32)
    col = jnp.tile(jnp.arange(GRID_W), rows).astype(F32)
    half_axis = ROPE_DIM // 2
    inv = ROPE_THETA ** (-jnp.arange(0, half_axis, 2, dtype=F32) / half_axis)
    ar = r[:, None] * inv
    ac = col[:, None] * inv
    cos = jnp.concatenate([jnp.cos(ar), jnp.cos(ar), jnp.cos(ac), jnp.cos(ac)], axis=-1)
    sin = jnp.concatenate([-jnp.sin(ar), jnp.sin(ar), -jnp.sin(ac), jnp.sin(ac)], axis=-1)
    reps = LANES // ROPE_DIM
    return jnp.tile(cos, (1, reps)), jnp.tile(sin, (1, reps))


def kernel(x_prompt, x_sample, cache_ckv, cache_kpe, c, c_ctx, w_mod, b_mod, g_norm1, g_norm2, w_in, conv_a, w_a_out, g_q, w_uq, g_kv, w_ukv, w_mla_out, conv_c, ln_c_g, ln_c_b, w_c_out, w_o, w_up, conv_f, w_down, g_final):
    x = jnp.concatenate([x_prompt.reshape(CTX_TOK, D_MODEL), x_sample.reshape(DEC_TOK, D_MODEL)], axis=0)
    cond = jnp.concatenate([c_ctx[None, :], c, jnp.zeros((COND_ROWS - N_COND, D_MODEL), F32)], axis=0)
    mod = _modulation(cond, w_mod, b_mod)
    cos, sin = _rope_tables()

    w_in_t = jnp.swapaxes(w_in, 1, 2)
    wq = w_uq.reshape(DEPTH, Q_RANK, N_HEADS, QK_DIM)
    w_q = jnp.concatenate([wq[..., :NOPE_DIM].reshape(DEPTH, Q_RANK, N_HEADS * NOPE_DIM),
                           wq[..., NOPE_DIM:].reshape(DEPTH, Q_RANK, N_HEADS * ROPE_DIM)],
                          axis=2).astype(BF16)
    kpe_cache = jnp.pad(cache_kpe, ((0, 0), (0, 0), (0, 0), (0, LANES - ROPE_DIM)))

    ckv_layers, kpe_layers = [], []
    for l in range(DEPTH):
        sh1, sc1, gt1, sh2, sc2, gt2 = [
            mod[l, :N_COND, k * D_MODEL:(k + 1) * D_MODEL].reshape(N_COND, 1, D_MODEL) for k in range(6)]

        h = _norm_mod(x, g_norm1[l][None, :], sc1, sh1)
        ya = _mixer_a(h, w_in_t, conv_a, l)
        q = _q_proj(h, w_in_t, g_q[l][None, :], w_q, l)
        ckv, kpe, kv = _kv_proj(h, w_in_t, g_kv[l][None, :], w_ukv, l)
        tail = _tail_proj(h, w_in_t, l)
        kv_cache = _cache_kv_proj(cache_ckv, w_ukv, l)
        attn = _attention(q, kv, kpe, kv_cache, kpe_cache[:, l], cos, sin)
        yc = _conv_c(tail, conv_c, ln_c_g[l][None, :], ln_c_b[l][None, :], l)
        merged = _merge(ya, w_a_out, attn, w_mla_out, yc, w_c_out, tail, l)
        x = _matmul_residual(merged, w_o, l, x, gt1, tn=512, tm=TMW, a_buffers=2, name="out_proj")

        h = _norm_mod(x, g_norm2[l][None, :], sc2, sh2)
        act = _ffn_up(h, w_up, conv_f, l)
        x = _matmul_residual(act, w_down, l, x, gt2, tn=256, tm=TMW, a_buffers=1, name="ffn_down")

        ckv_layers.append(ckv[:CTX_TOK].reshape(BATCH, SEQ, KV_RANK))
        kpe_layers.append(kpe[:CTX_TOK, :ROPE_DIM].reshape(BATCH, SEQ, ROPE_DIM))

    g = g_final[None, :]
    y_prompt = _final_norm(x, g, 0, CTX_TOK).reshape(BATCH, SEQ, D_MODEL)
    y_sample = _final_norm(x, g, CTX_TOK, DEC_TOK).reshape(DEC_BATCH, DEC_SEQ, D_MODEL)
    return (y_prompt, y_sample, jnp.stack(ckv_layers, axis=1), jnp.stack(kpe_layers, axis=1))
```

```python
import math

import jax
import jax.numpy as jnp
from jax import lax
from jax.experimental import pallas as pl
from jax.experimental.pallas import tpu as pltpu

D_MODEL = 2048
BATCH = 16
SEQ = 256
DEPTH = 2
DEC_BATCH = 2
DEC_SEQ = 1024
PAST_LEN = 256
GRID_W = 64
A_W = 1024
A_KERNEL = 3
N_HEADS = 16
Q_RANK = 768
KV_RANK = 512
NOPE_DIM = 128
ROPE_DIM = 64
V_DIM = 128
QK_DIM = NOPE_DIM + ROPE_DIM
C_W = 1024
C_KERNEL = 31
D_FF = 5632
F_KERNEL = 3
ROPE_THETA = 10000.0
EPS = 1e-6

CTX_TOK = BATCH * SEQ
DEC_TOK = DEC_BATCH * DEC_SEQ
TOK = CTX_TOK + DEC_TOK
TM = 1024
N_ROW_BLOCKS = TOK // TM
CTX_BLOCKS = CTX_TOK // TM
TMW = 2048
N_COND = 3
COND_ROWS = 8
LANES = 128
SUBLANES = 8
HEAD_PAIRS = N_HEADS // 2
KV_A_W = 640
CONV_PAD = 16
CONV_ROWS = 64
NORM_ROWS = 64
TQ = 256
VMEM_LIMIT = 56 * 1024 * 1024

IN_AH, IN_AB, IN_AC = 0, A_W, 2 * A_W
IN_Q = 3 * A_W
IN_KV = IN_Q + Q_RANK
IN_GLU = IN_KV + KV_RANK + ROPE_DIM
TAIL_G1, TAIL_G2, TAIL_GATE = 0, C_W, 2 * C_W
TAIL_W = 2 * C_W + 3 * D_MODEL

F32 = jnp.float32
BF16 = jnp.bfloat16


def _params(n_axes):
    return pltpu.CompilerParams(dimension_semantics=("arbitrary",) * n_axes,
                                vmem_limit_bytes=VMEM_LIMIT)


def _cond_of_row(row):
    return jnp.maximum(row // DEC_SEQ - (CTX_TOK // DEC_SEQ - 1), 0)


def _cond_row(i):
    return _cond_of_row(i * TM)


def _sigmoid(x):
    return 0.5 * jnp.tanh(0.5 * x) + 0.5


def _silu(x):
    return x * _sigmoid(x)


def _bdot(a, w):
    return jnp.dot(a, w.astype(BF16), preferred_element_type=F32)


def _bdot_t(a, wt):
    return lax.dot_general(a, wt.astype(BF16), (((1,), (1,)), ((), ())), preferred_element_type=F32)


def _conv3_rows(u, w, is_ctx):
    n, width = u.shape
    inside = jnp.where(is_ctx, 0.0, 1.0)
    sub = lax.broadcasted_iota(jnp.int32, (SUBLANES, width), 0)
    zeros = jnp.zeros((SUBLANES, width), F32)
    pieces = []
    for r0 in range(0, n, SEQ):
        up = u[r0:r0 + SEQ]
        prev = pltpu.roll(up, 1, 0)
        nxt = pltpu.roll(up, SEQ - 1, 0)
        before = pltpu.roll(u[r0 - SUBLANES:r0], 1, 0) * inside if r0 % DEC_SEQ else zeros
        after = (pltpu.roll(u[r0 + SEQ:r0 + SEQ + SUBLANES], SUBLANES - 1, 0) * inside
                 if (r0 + SEQ) % DEC_SEQ else zeros)
        prev = jnp.concatenate([jnp.where(sub == 0, before, prev[:SUBLANES]), prev[SUBLANES:]], axis=0)
        nxt = jnp.concatenate([nxt[:SEQ - SUBLANES],
                               jnp.where(sub == SUBLANES - 1, after, nxt[SEQ - SUBLANES:])], axis=0)
        pieces.append(w[0:1, :] * prev + w[1:2, :] * up + w[2:3, :] * nxt)
    return jnp.concatenate(pieces, axis=0)


def _mod_kernel(c_ref, w_ref, b_ref, o_ref):
    s = _silu(c_ref[...]).astype(BF16)
    o_ref[...] = _bdot(s, w_ref[...]) + b_ref[...]


def _modulation(cond, w_mod, b_mod):
    tn = 1024
    n = 6 * D_MODEL
    return pl.pallas_call(
        _mod_kernel,
        grid=(DEPTH, n // tn),
        in_specs=[pl.BlockSpec((COND_ROWS, D_MODEL), lambda l, j: (0, 0)),
                  pl.BlockSpec((None, D_MODEL, tn), lambda l, j: (l, 0, j)),
                  pl.BlockSpec((None, 1, tn), lambda l, j: (l, 0, j))],
        out_specs=pl.BlockSpec((None, COND_ROWS, tn), lambda l, j: (l, 0, j)),
        out_shape=jax.ShapeDtypeStruct((DEPTH, COND_ROWS, n), F32),
        compiler_params=_params(2),
        name="modulation",
    )(cond, w_mod, b_mod.reshape(DEPTH, 1, n))


def _rms(x):
    return x * lax.rsqrt(jnp.mean(x * x, axis=-1, keepdims=True) + EPS)


def _norm_mod_kernel(x_ref, g_ref, sc_ref, sh_ref, o_ref):
    y = _rms(x_ref[...]) * g_ref[...]
    o_ref[...] = (y * (1.0 + sc_ref[...]) + sh_ref[...]).astype(o_ref.dtype)


def _norm_mod(x, g, sc, sh):
    tr = 512
    per = TM // tr
    cond = lambda i: (_cond_row(i // per), 0, 0)
    return pl.pallas_call(
        _norm_mod_kernel,
        grid=(TOK // tr,),
        in_specs=[pl.BlockSpec((tr, D_MODEL), lambda i: (i, 0)),
                  pl.BlockSpec((1, D_MODEL), lambda i: (0, 0)),
                  pl.BlockSpec((None, 1, D_MODEL), cond),
                  pl.BlockSpec((None, 1, D_MODEL), cond)],
        out_specs=pl.BlockSpec((tr, D_MODEL), lambda i: (i, 0)),
        out_shape=jax.ShapeDtypeStruct((TOK, D_MODEL), BF16),
        compiler_params=_params(1),
        name="norm_mod",
    )(x, g, sc, sh)


def _final_norm_kernel(x_ref, g_ref, o_ref):
    o_ref[...] = _rms(x_ref[...]) * g_ref[...]


def _final_norm(x, g, row0, rows):
    tr = 512
    return pl.pallas_call(
        _final_norm_kernel,
        grid=(rows // tr,),
        in_specs=[pl.BlockSpec((tr, D_MODEL), lambda i: (row0 // tr + i, 0)),
                  pl.BlockSpec((1, D_MODEL), lambda i: (0, 0))],
        out_specs=pl.BlockSpec((tr, D_MODEL), lambda i: (i, 0)),
        out_shape=jax.ShapeDtypeStruct((rows, D_MODEL), F32),
        compiler_params=_params(1),
        name="final_norm",
    )(x, g)


def _mm_residual_kernel(a_ref, w_ref, x_ref, gt_lo_ref, gt_hi_ref, o_ref):
    acc = _bdot(a_ref[...], w_ref[...])
    half = acc.shape[0] // 2
    o_ref[:half, :] = x_ref[:half, :] + gt_lo_ref[...] * acc[:half]
    o_ref[half:, :] = x_ref[half:, :] + gt_hi_ref[...] * acc[half:]


def _matmul_residual(a, w, l, x, gt, tn, tm, a_buffers, name):
    m, k = a.shape
    n = w.shape[2]
    return pl.pallas_call(
        _mm_residual_kernel,
        grid=(m // tm, n // tn),
        in_specs=[pl.BlockSpec((tm, k), lambda i, j: (i, 0), pipeline_mode=pl.Buffered(a_buffers)),
                  pl.BlockSpec((None, k, tn), lambda i, j: (l, 0, j)),
                  pl.BlockSpec((tm, tn), lambda i, j: (i, j)),
                  pl.BlockSpec((None, 1, tn), lambda i, j: (_cond_of_row(i * tm), 0, j)),
                  pl.BlockSpec((None, 1, tn), lambda i, j: (_cond_of_row(i * tm + tm // 2), 0, j))],
        out_specs=pl.BlockSpec((tm, tn), lambda i, j: (i, j)),
        out_shape=jax.ShapeDtypeStruct((m, n), F32),
        compiler_params=_params(2),
        name=name,
    )(a, w, x, gt, gt)


def _tail_kernel(h_ref, wt_ref, o_ref):
    o_ref[...] = _bdot_t(h_ref[...], wt_ref[...]).astype(o_ref.dtype)


def _tail_proj(h, w_in_t, l):
    tn = 512
    in_w = w_in_t.shape[1]
    return pl.pallas_call(
        _tail_kernel,
        grid=(TOK // TMW, TAIL_W // tn),
        in_specs=[pl.BlockSpec((TMW, D_MODEL), lambda i, j: (i, 0)),
                  pl.BlockSpec((pl.Element(tn), pl.Element(D_MODEL)),
                               lambda i, j: (pl.multiple_of(l * in_w + IN_GLU + j * tn, SUBLANES), 0))],
        out_specs=pl.BlockSpec((TMW, tn), lambda i, j: (i, j)),
        out_shape=jax.ShapeDtypeStruct((TOK, TAIL_W), BF16),
        compiler_params=_params(2),
        name="tail_proj",
    )(h, w_in_t.reshape(DEPTH * in_w, D_MODEL))


def _mixer_a_kernel(h_ref, wh_ref, wb_ref, wc_ref, cw_ref, o_ref):
    is_ctx = pl.program_id(0) < CTX_TOK // TMW
    h = h_ref[...]
    p = _bdot_t(h, wc_ref[...]) * _bdot_t(h, wh_ref[...])
    o_ref[...] = (_bdot_t(h, wb_ref[...]) * _conv3_rows(p, cw_ref[...], is_ctx)).astype(o_ref.dtype)


def _mixer_a(h, w_in, conv_a, l):
    tn = 256
    col = lambda off: (lambda i, j: (l, off // tn + j, 0))
    wspec = lambda off: pl.BlockSpec((None, tn, D_MODEL), col(off))
    return pl.pallas_call(
        _mixer_a_kernel,
        grid=(TOK // TMW, A_W // tn),
        in_specs=[pl.BlockSpec((TMW, D_MODEL), lambda i, j: (i, 0)),
                  wspec(IN_AH), wspec(IN_AB), wspec(IN_AC),
                  pl.BlockSpec((None, A_KERNEL, tn), lambda i, j: (l, 0, j))],
        out_specs=pl.BlockSpec((TMW, tn), lambda i, j: (i, j)),
        out_shape=jax.ShapeDtypeStruct((TOK, A_W), BF16),
        compiler_params=_params(2),
        name="mixer_a",
    )(h, w_in, w_in, w_in, conv_a)


def _q_kernel(h_ref, wa_ref, g_ref, w_ref, o_ref, qn_ref):
    @pl.when(pl.program_id(1) == 0)
    def _():
        qn_ref[...] = (_rms(_bdot_t(h_ref[...], wa_ref[...])) * g_ref[...]).astype(BF16)

    o_ref[...] = _bdot(qn_ref[...], w_ref[...]).astype(o_ref.dtype)


def _q_proj(h, w_in, g_q, w_q, l):
    tn = 1024
    n = w_q.shape[2]
    return pl.pallas_call(
        _q_kernel,
        grid=(TOK // TMW, n // tn),
        in_specs=[pl.BlockSpec((TMW, D_MODEL), lambda i, j: (i, 0)),
                  pl.BlockSpec((None, Q_RANK, D_MODEL), lambda i, j: (l, IN_Q // Q_RANK, 0),
                               pipeline_mode=pl.Buffered(1)),
                  pl.BlockSpec((1, Q_RANK), lambda i, j: (0, 0)),
                  pl.BlockSpec((None, Q_RANK, tn), lambda i, j: (l, 0, j))],
        out_specs=pl.BlockSpec((TMW, tn), lambda i, j: (i, j)),
        out_shape=jax.ShapeDtypeStruct((TOK, n), BF16),
        scratch_shapes=[pltpu.VMEM((TMW, Q_RANK), BF16)],
        compiler_params=_params(2),
        name="q_proj",
    )(h, w_in, g_q, w_q)


def _kv_kernel(h_ref, wa_ref, g_ref, w_ref, ckv_ref, kpe_ref, kv_ref, cn_ref):
    @pl.when(pl.program_id(1) == 0)
    def _():
        kv_a = _bdot_t(h_ref[...], wa_ref[...])
        y = _rms(kv_a[:, :KV_RANK]) * g_ref[...]
        ckv_ref[...] = y
        kpe_ref[...] = kv_a[:, KV_RANK:]
        cn_ref[...] = y.astype(BF16)

    kv_ref[...] = _bdot(cn_ref[...], w_ref[...]).astype(kv_ref.dtype)


def _kv_proj(h, w_in, g_kv, w_ukv, l):
    tn = 512
    n = w_ukv.shape[2]
    return pl.pallas_call(
        _kv_kernel,
        grid=(TOK // TMW, n // tn),
        in_specs=[pl.BlockSpec((TMW, D_MODEL), lambda i, j: (i, 0)),
                  pl.BlockSpec((None, KV_A_W, D_MODEL), lambda i, j: (l, IN_KV // KV_A_W, 0),
                               pipeline_mode=pl.Buffered(1)),
                  pl.BlockSpec((1, KV_RANK), lambda i, j: (0, 0)),
                  pl.BlockSpec((None, KV_RANK, tn), lambda i, j: (l, 0, j))],
        out_specs=[pl.BlockSpec((TMW, KV_RANK), lambda i, j: (i, 0)),
                   pl.BlockSpec((TMW, LANES), lambda i, j: (i, 0)),
                   pl.BlockSpec((TMW, tn), lambda i, j: (i, j))],
        out_shape=[jax.ShapeDtypeStruct((TOK, KV_RANK), F32),
                   jax.ShapeDtypeStruct((TOK, LANES), F32),
                   jax.ShapeDtypeStruct((TOK, n), BF16)],
        scratch_shapes=[pltpu.VMEM((TMW, KV_RANK), BF16)],
        compiler_params=_params(2),
        name="kv_proj",
    )(h, w_in, g_kv, w_ukv)


def _cache_kv_kernel(a_ref, w_ref, o_ref):
    o_ref[...] = _bdot(a_ref[...].astype(BF16), w_ref[...]).astype(o_ref.dtype)


def _cache_kv_proj(cache_ckv, w_ukv, l):
    tn = 1024
    n = w_ukv.shape[2]
    return pl.pallas_call(
        _cache_kv_kernel,
        grid=(DEC_BATCH, n // tn),
        in_specs=[pl.BlockSpec((None, None, PAST_LEN, KV_RANK), lambda b, j: (b, l, 0, 0)),
                  pl.BlockSpec((None, KV_RANK, tn), lambda b, j: (l, 0, j))],
        out_specs=pl.BlockSpec((PAST_LEN, tn), lambda b, j: (b, j)),
        out_shape=jax.ShapeDtypeStruct((DEC_BATCH * PAST_LEN, n), BF16),
        compiler_params=_params(2),
        name="cache_kv_proj",
    )(cache_ckv, w_ukv)


def _merge_kernel(a1_ref, w1_ref, a2_ref, w2_ref, a3_ref, w3_ref, g1_ref, g2_ref, g3_ref, o_ref):
    def branch(a_ref, w_ref, g_ref):
        return _sigmoid(g_ref[...].astype(F32)) * _bdot(a_ref[...], w_ref[...])

    o_ref[...] = (branch(a1_ref, w1_ref, g1_ref) + branch(a2_ref, w2_ref, g2_ref)
                  + branch(a3_ref, w3_ref, g3_ref)).astype(o_ref.dtype)


def _merge(ya, w_a_out, attn, w_mla_out, yc, w_c_out, tail, l):
    tn = 256
    gate = lambda k: (lambda i, j: (i, (TAIL_GATE + k * D_MODEL) // tn + j))
    act = lambda width: pl.BlockSpec((TM, width), lambda i, j: (i, 0))
    wgt = lambda width: pl.BlockSpec((None, width, tn), lambda i, j: (l, 0, j))
    return pl.pallas_call(
        _merge_kernel,
        grid=(N_ROW_BLOCKS, D_MODEL // tn),
        in_specs=[act(A_W), wgt(A_W), act(N_HEADS * V_DIM), wgt(N_HEADS * V_DIM), act(C_W), wgt(C_W),
                  pl.BlockSpec((TM, tn), gate(0)), pl.BlockSpec((TM, tn), gate(1)),
                  pl.BlockSpec((TM, tn), gate(2))],
        out_specs=pl.BlockSpec((TM, tn), lambda i, j: (i, j)),
        out_shape=jax.ShapeDtypeStruct((TOK, D_MODEL), BF16),
        compiler_params=_params(2),
        name="merge",
    )(ya, w_a_out, attn, w_mla_out, yc, w_c_out, tail, tail, tail)


FFN_TN = 512
FFN_COL_TILES = D_FF // FFN_TN


def _ffn_up_kernel(h_ref, wa_ref, wg_ref, ca_ref, cg_ref, o_ref):
    is_ctx = pl.program_id(0) < CTX_BLOCKS
    h = h_ref[...]
    ua = _bdot(h, wa_ref[...])
    ug = _bdot(h, wg_ref[...])
    a = _conv3_rows(ua, ca_ref[...], is_ctx)
    g = _conv3_rows(ug, cg_ref[...], is_ctx)
    o_ref[...] = (_silu(a) * g).astype(o_ref.dtype)


def _ffn_up(h, w_up, conv_f, l):
    tn = FFN_TN
    half = FFN_COL_TILES
    return pl.pallas_call(
        _ffn_up_kernel,
        grid=(N_ROW_BLOCKS, half),
        in_specs=[pl.BlockSpec((TM, D_MODEL), lambda i, j: (i, 0)),
                  pl.BlockSpec((None, D_MODEL, tn), lambda i, j: (l, 0, j)),
                  pl.BlockSpec((None, D_MODEL, tn), lambda i, j: (l, 0, half + j)),
                  pl.BlockSpec((None, F_KERNEL, tn), lambda i, j: (l, 0, j)),
                  pl.BlockSpec((None, F_KERNEL, tn), lambda i, j: (l, 0, half + j))],
        out_specs=pl.BlockSpec((TM, tn), lambda i, j: (i, j)),
        out_shape=jax.ShapeDtypeStruct((TOK, D_FF), BF16),
        compiler_params=_params(2),
        name="ffn_up",
    )(h, w_up, w_up, conv_f, conv_f)


def _conv_c_kernel(g1_ref, g2_ref, w_ref, lng_ref, lnb_ref, o_ref, u_ref, sh_ref, cv_ref):
    is_dec = pl.program_id(0) >= CTX_BLOCKS
    halo_w = SEQ + 2 * CONV_PAD
    n_pieces = TM // SEQ

    def gated(r0, n):
        return g1_ref[r0:r0 + n, :].astype(F32) * _sigmoid(g2_ref[r0:r0 + n, :].astype(F32))

    zeros = jnp.zeros((CONV_PAD, C_W), F32)
    u_ref[halo_w:halo_w + SUBLANES, :] = zeros[:SUBLANES]

    for p in range(n_pieces):
        r0 = p * SEQ
        top = jnp.where(is_dec, gated(r0 - CONV_PAD, CONV_PAD), 0.0) if p > 0 else zeros
        bot = jnp.where(is_dec, gated(r0 + SEQ, CONV_PAD), 0.0) if p < n_pieces - 1 else zeros
        u_ref[0:CONV_PAD, :] = top
        u_ref[CONV_PAD:CONV_PAD + SEQ, :] = gated(r0, SEQ)
        u_ref[CONV_PAD + SEQ:halo_w, :] = bot

        def shift_rows(c, carry):
            base = pl.multiple_of(c * SUBLANES, SUBLANES)
            win = u_ref[pl.ds(base, 2 * SUBLANES), :]
            sh_ref[0, pl.ds(base, SUBLANES), :] = win[:SUBLANES]
            for b in range(1, SUBLANES):
                sh_ref[b, pl.ds(base, SUBLANES), :] = pltpu.roll(win, 2 * SUBLANES - b, 0)[:SUBLANES]
            return carry

        lax.fori_loop(0, halo_w // SUBLANES, shift_rows, 0)

        for s in range(C_W // LANES):
            cols = slice(s * LANES, (s + 1) * LANES)
            taps = [jnp.broadcast_to(w_ref[k:k + 1, cols], (SUBLANES, LANES)) for k in range(C_KERNEL)]

            def conv_rows(c, carry, cols=cols, taps=taps):
                base = pl.multiple_of(c * CONV_ROWS, CONV_ROWS)
                for r in range(0, CONV_ROWS, SUBLANES):
                    acc = None
                    for k in range(C_KERNEL):
                        off = CONV_PAD - C_KERNEL // 2 + k
                        start = pl.multiple_of(base + r + (off // SUBLANES) * SUBLANES, SUBLANES)
                        term = taps[k] * sh_ref[off % SUBLANES, pl.ds(start, SUBLANES), cols]
                        acc = term if acc is None else acc + term
                    cv_ref[pl.ds(base + r, SUBLANES), cols] = acc
                return carry

            lax.fori_loop(0, SEQ // CONV_ROWS, conv_rows, 0)

        for base in range(0, SEQ, NORM_ROWS):
            v = cv_ref[base:base + NORM_ROWS, :]
            d = v - jnp.mean(v, axis=-1, keepdims=True)
            var = jnp.mean(d * d, axis=-1, keepdims=True)
            y = d * lax.rsqrt(var + EPS) * lng_ref[...] + lnb_ref[...]
            o_ref[r0 + base:r0 + base + NORM_ROWS, :] = _silu(y).astype(o_ref.dtype)


def _conv_c(tail, conv_c, ln_g, ln_b, l):
    halo_w = SEQ + 2 * CONV_PAD
    return pl.pallas_call(
        _conv_c_kernel,
        grid=(N_ROW_BLOCKS,),
        in_specs=[pl.BlockSpec((TM, C_W), lambda i: (i, TAIL_G1 // C_W)),
                  pl.BlockSpec((TM, C_W), lambda i: (i, TAIL_G2 // C_W)),
                  pl.BlockSpec((None, C_KERNEL, C_W), lambda i: (l, 0, 0)),
                  pl.BlockSpec((1, C_W), lambda i: (0, 0)),
                  pl.BlockSpec((1, C_W), lambda i: (0, 0))],
        out_specs=pl.BlockSpec((TM, C_W), lambda i: (i, 0)),
        out_shape=jax.ShapeDtypeStruct((TOK, C_W), BF16),
        scratch_shapes=[pltpu.VMEM((halo_w + SUBLANES, C_W), F32),
                        pltpu.VMEM((SUBLANES, halo_w, C_W), F32),
                        pltpu.VMEM((SEQ, C_W), F32)],
        compiler_params=_params(1),
        name="conv_c",
    )(tail, tail, conv_c, ln_g, ln_b)


def _rope_pairs(x, cos, sin):
    half = ROPE_DIM // 4
    lane = lax.broadcasted_iota(jnp.int32, x.shape, 1)
    swapped = jnp.where((lane & (2 * half - 1)) < half,
                        pltpu.roll(x, LANES - half, 1), pltpu.roll(x, half, 1))
    return x * cos + swapped * sin


def _pe_keys(k):
    lane = lax.broadcasted_iota(jnp.int32, k.shape, 1)
    lo = jnp.where(lane < ROPE_DIM, k, 0.0)
    return lo.astype(BF16), pltpu.roll(lo, ROPE_DIM, 1).astype(BF16)


def _attend(q_ref, segments, o_ref, rope_q=None):
    c = QK_DIM ** -0.5 * math.log2(math.e)
    pe_col0 = N_HEADS * NOPE_DIM
    for hp in range(HEAD_PAIRS):
        qp = q_ref[:, pe_col0 + hp * LANES:pe_col0 + (hp + 1) * LANES]
        if rope_q is not None:
            qp = _rope_pairs(qp.astype(F32), *rope_q).astype(BF16)
        for h in range(2):
            head = 2 * hp + h
            q = jnp.concatenate([q_ref[:, head * NOPE_DIM:(head + 1) * NOPE_DIM], qp], axis=1)
            kbase = head * (NOPE_DIM + V_DIM)
            scores = []
            for kv_ref, pe in segments:
                k = jnp.concatenate([kv_ref[:, kbase:kbase + NOPE_DIM], pe[h]], axis=1)
                scores.append(lax.dot_general(q, k, (((1,), (1,)), ((), ())),
                                              preferred_element_type=F32))
            m = scores[0].max(axis=-1, keepdims=True)
            for s in scores[1:]:
                m = jnp.maximum(m, s.max(axis=-1, keepdims=True))
            denom = 0.0
            out = 0.0
            for s, (kv_ref, _) in zip(scores, segments):
                p = jnp.exp2((s - m) * c)
                denom = denom + p.sum(axis=-1, keepdims=True)
                out = out + jnp.dot(p.astype(BF16), kv_ref[:, kbase + NOPE_DIM:kbase + NOPE_DIM + V_DIM],
                                    preferred_element_type=F32)
            o_ref[:, head * V_DIM:(head + 1) * V_DIM] = (out / denom).astype(o_ref.dtype)


def _attn_kernel(q_ref, kvx_ref, kpex_ref, kvd_ref, kped_ref, kvc_ref, kpec_ref,
                 cq_ref, sq_ref, ck_ref, sk_ref, o_ref):
    step = pl.program_id(0)

    @pl.when(step < BATCH)
    def _():
        _attend(q_ref, [(kvx_ref, _pe_keys(kpex_ref[...]))], o_ref)

    @pl.when(step >= BATCH)
    def _():
        new_pe = _rope_pairs(kped_ref[...], ck_ref[...], sk_ref[...])
        _attend(q_ref, [(kvc_ref, _pe_keys(kpec_ref[...])), (kvd_ref, _pe_keys(new_pe))], o_ref,
                rope_q=(cq_ref[...], sq_ref[...]))


def _attention(q, kv, kpe, kv_cache, kpe_cache, cos, sin):
    per_seq = DEC_SEQ // TQ
    n_steps = BATCH + DEC_BATCH * per_seq
    ctx = lambda s: jnp.minimum(s, BATCH - 1)
    dec = lambda s: jnp.maximum(s - BATCH, 0)
    kv_w = N_HEADS * (NOPE_DIM + V_DIM)
    in_specs = [
        pl.BlockSpec((TQ, q.shape[1]), lambda s: (s, 0)),
        pl.BlockSpec((SEQ, kv_w), lambda s: (ctx(s), 0)),
        pl.BlockSpec((SEQ, LANES), lambda s: (ctx(s), 0)),
        pl.BlockSpec((DEC_SEQ, kv_w), lambda s: (CTX_TOK // DEC_SEQ + dec(s) // per_seq, 0)),
        pl.BlockSpec((DEC_SEQ, LANES), lambda s: (CTX_TOK // DEC_SEQ + dec(s) // per_seq, 0)),
        pl.BlockSpec((PAST_LEN, kv_w), lambda s: (dec(s) // per_seq, 0)),
        pl.BlockSpec((None, PAST_LEN, LANES), lambda s: (dec(s) // per_seq, 0, 0)),
        pl.BlockSpec((TQ, LANES), lambda s: (dec(s) % per_seq, 0)),
        pl.BlockSpec((TQ, LANES), lambda s: (dec(s) % per_seq, 0)),
        pl.BlockSpec((DEC_SEQ, LANES), lambda s: (0, 0)),
        pl.BlockSpec((DEC_SEQ, LANES), lambda s: (0, 0)),
    ]
    return pl.pallas_call(
        _attn_kernel,
        grid=(n_steps,),
        in_specs=in_specs,
        out_specs=pl.BlockSpec((TQ, N_HEADS * V_DIM), lambda s: (s, 0)),
        out_shape=jax.ShapeDtypeStruct((TOK, N_HEADS * V_DIM), BF16),
        compiler_params=_params(1),
        name="attention",
    )(q, kv, kpe, kv, kpe, kv_cache, kpe_cache, cos, sin, cos, sin)


def _rope_tables():
    rows = DEC_SEQ // GRID_W
    r = jnp.repeat(jnp.arange(rows), GRID_W).astype(F32)
    col = jnp.tile(jnp.arange(GRID_W), rows).astype(F32)
    half_axis = ROPE_DIM // 2
    inv = ROPE_THETA ** (-jnp.arange(0, half_axis, 2, dtype=F32) / half_axis)
    ar = r[:, None] * inv
    ac = col[:, None] * inv
    cos = jnp.concatenate([jnp.cos(ar), jnp.cos(ar), jnp.cos(ac), jnp.cos(ac)], axis=-1)
    sin = jnp.concatenate([-jnp.sin(ar), jnp.sin(ar), -jnp.sin(ac), jnp.sin(ac)], axis=-1)
    reps = LANES // ROPE_DIM
    return jnp.tile(cos, (1, reps)), jnp.tile(sin, (1, reps))


def kernel(x_prompt, x_sample, cache_ckv, cache_kpe, c, c_ctx, w_mod, b_mod, g_norm1, g_norm2, w_in, conv_a, w_a_out, g_q, w_uq, g_kv, w_ukv, w_mla_out, conv_c, ln_c_g, ln_c_b, w_c_out, w_o, w_up, conv_f, w_down, g_final):
    x = jnp.concatenate([x_prompt.reshape(CTX_TOK, D_MODEL), x_sample.reshape(DEC_TOK, D_MODEL)], axis=0)
    cond = jnp.concatenate([c_ctx[None, :], c, jnp.zeros((COND_ROWS - N_COND, D_MODEL), F32)], axis=0)
    mod = _modulation(cond, w_mod, b_mod)
    cos, sin = _rope_tables()

    w_in_t = jnp.swapaxes(w_in, 1, 2)
    wq = w_uq.reshape(DEPTH, Q_RANK, N_HEADS, QK_DIM)
    w_q = jnp.concatenate([wq[..., :NOPE_DIM].reshape(DEPTH, Q_RANK, N_HEADS * NOPE_DIM),
                           wq[..., NOPE_DIM:].reshape(DEPTH, Q_RANK, N_HEADS * ROPE_DIM)],
                          axis=2).astype(BF16)
    kpe_cache = jnp.pad(cache_kpe, ((0, 0), (0, 0), (0, 0), (0, LANES - ROPE_DIM)))

    ckv_layers, kpe_layers = [], []
    for l in range(DEPTH):
        sh1, sc1, gt1, sh2, sc2, gt2 = [
            mod[l, :N_COND, k * D_MODEL:(k + 1) * D_MODEL].reshape(N_COND, 1, D_MODEL) for k in range(6)]

        h = _norm_mod(x, g_norm1[l][None, :], sc1, sh1)
        ya = _mixer_a(h, w_in_t, conv_a, l)
        q = _q_proj(h, w_in_t, g_q[l][None, :], w_q, l)
        ckv, kpe, kv = _kv_proj(h, w_in_t, g_kv[l][None, :], w_ukv, l)
        tail = _tail_proj(h, w_in_t, l)
        kv_cache = _cache_kv_proj(cache_ckv, w_ukv, l)
        attn = _attention(q, kv, kpe, kv_cache, kpe_cache[:, l], cos, sin)
        yc = _conv_c(tail, conv_c, ln_c_g[l][None, :], ln_c_b[l][None, :], l)
        merged = _merge(ya, w_a_out, attn, w_mla_out, yc, w_c_out, tail, l)
        x = _matmul_residual(merged, w_o, l, x, gt1, tn=512, tm=TMW, a_buffers=2, name="out_proj")

        h = _norm_mod(x, g_norm2[l][None, :], sc2, sh2)
        act = _ffn_up(h, w_up, conv_f, l)
        x = _matmul_residual(act, w_down, l, x, gt2, tn=256, tm=TM, a_buffers=2, name="ffn_down")

        ckv_layers.append(ckv[:CTX_TOK].reshape(BATCH, SEQ, KV_RANK))
        kpe_layers.append(kpe[:CTX_TOK, :ROPE_DIM].reshape(BATCH, SEQ, ROPE_DIM))

    g = g_final[None, :]
    y_prompt = _final_norm(x, g, 0, CTX_TOK).reshape(BATCH, SEQ, D_MODEL)
    y_sample = _final_norm(x, g, CTX_TOK, DEC_TOK).reshape(DEC_BATCH, DEC_SEQ, D_MODEL)
    return (y_prompt, y_sample, jnp.stack(ckv_layers, axis=1), jnp.stack(kpe_layers, axis=1))
```

```python
import functools
import math

import jax
import jax.numpy as jnp
from jax import lax
from jax.experimental import pallas as pl
from jax.experimental.pallas import tpu as pltpu

D_MODEL = 2048
BATCH = 16
SEQ = 256
DEPTH = 2
DEC_BATCH = 2
DEC_SEQ = 1024
PAST_LEN = 256
GRID_W = 64
A_W = 1024
A_KERNEL = 3
N_HEADS = 16
Q_RANK = 768
KV_RANK = 512
NOPE_DIM = 128
ROPE_DIM = 64
V_DIM = 128
QK_DIM = NOPE_DIM + ROPE_DIM
C_W = 1024
C_KERNEL = 31
D_FF = 5632
F_KERNEL = 3
ROPE_THETA = 10000.0
EPS = 1e-6

CTX_TOK = BATCH * SEQ
DEC_TOK = DEC_BATCH * DEC_SEQ
TOK = CTX_TOK + DEC_TOK
TM = 1024
N_ROW_BLOCKS = TOK // TM
CTX_BLOCKS = CTX_TOK // TM
TMW = 2048
N_COND = 3
COND_ROWS = 8
LANES = 128
SUBLANES = 8
HEAD_PAIRS = N_HEADS // 2
KV_A_W = 640
CONV_PAD = 16
CONV_ROWS = 64
NORM_ROWS = 64
TQ = 256
VMEM_LIMIT = 56 * 1024 * 1024

IN_AH, IN_AB, IN_AC = 0, A_W, 2 * A_W
IN_Q = 3 * A_W
IN_KV = IN_Q + Q_RANK
IN_GLU = IN_KV + KV_RANK + ROPE_DIM
TAIL_G1, TAIL_G2, TAIL_GATE = 0, C_W, 2 * C_W
TAIL_W = 2 * C_W + 3 * D_MODEL

F32 = jnp.float32
BF16 = jnp.bfloat16


def _params(n_axes):
    return pltpu.CompilerParams(dimension_semantics=("arbitrary",) * n_axes,
                                vmem_limit_bytes=VMEM_LIMIT)


def _cond_of_row(row):
    return jnp.maximum(row // DEC_SEQ - (CTX_TOK // DEC_SEQ - 1), 0)


def _sigmoid(x):
    return 0.5 * jnp.tanh(0.5 * x) + 0.5


def _silu(x):
    return x * _sigmoid(x)


def _bdot(a, w):
    return jnp.dot(a, w.astype(BF16), preferred_element_type=F32)


def _bdot_t(a, wt):
    return lax.dot_general(a, wt.astype(BF16), (((1,), (1,)), ((), ())), preferred_element_type=F32)


def _conv3_rows(u, w, is_ctx):
    n, width = u.shape
    inside = jnp.where(is_ctx, 0.0, 1.0)
    sub = lax.broadcasted_iota(jnp.int32, (SUBLANES, width), 0)
    zeros = jnp.zeros((SUBLANES, width), F32)
    pieces = []
    for r0 in range(0, n, SEQ):
        up = u[r0:r0 + SEQ]
        prev = pltpu.roll(up, 1, 0)
        nxt = pltpu.roll(up, SEQ - 1, 0)
        before = pltpu.roll(u[r0 - SUBLANES:r0], 1, 0) * inside if r0 % DEC_SEQ else zeros
        after = (pltpu.roll(u[r0 + SEQ:r0 + SEQ + SUBLANES], SUBLANES - 1, 0) * inside
                 if (r0 + SEQ) % DEC_SEQ else zeros)
        prev = jnp.concatenate([jnp.where(sub == 0, before, prev[:SUBLANES]), prev[SUBLANES:]], axis=0)
        nxt = jnp.concatenate([nxt[:SEQ - SUBLANES],
                               jnp.where(sub == SUBLANES - 1, after, nxt[SEQ - SUBLANES:])], axis=0)
        pieces.append(w[0:1, :] * prev + w[1:2, :] * up + w[2:3, :] * nxt)
    return jnp.concatenate(pieces, axis=0)


def _mod_kernel(c_ref, w_ref, b_ref, o_ref):
    s = _silu(c_ref[...]).astype(BF16)
    o_ref[...] = _bdot(s, w_ref[...]) + b_ref[...]


def _modulation(cond, w_mod, b_mod):
    tn = 1024
    n = 6 * D_MODEL
    return pl.pallas_call(
        _mod_kernel,
        grid=(DEPTH, n // tn),
        in_specs=[pl.BlockSpec((COND_ROWS, D_MODEL), lambda l, j: (0, 0)),
                  pl.BlockSpec((None, D_MODEL, tn), lambda l, j: (l, 0, j)),
                  pl.BlockSpec((None, 1, tn), lambda l, j: (l, 0, j))],
        out_specs=pl.BlockSpec((None, COND_ROWS, tn), lambda l, j: (l, 0, j)),
        out_shape=jax.ShapeDtypeStruct((DEPTH, COND_ROWS, n), F32),
        compiler_params=_params(2),
        name="modulation",
    )(cond, w_mod, b_mod.reshape(DEPTH, 1, n))


def _rms(x):
    return x * lax.rsqrt(jnp.mean(x * x, axis=-1, keepdims=True) + EPS)


def _stream_specs(xs, block, col):
    if len(xs) == 1:
        return [pl.BlockSpec(block, lambda i, *j: (i, col(*j)))], None
    n_first = xs[0].shape[0] // block[0]
    return [pl.BlockSpec(block, lambda i, *j: (jnp.minimum(i, n_first - 1), col(*j))),
            pl.BlockSpec(block, lambda i, *j: (jnp.maximum(i - n_first, 0), col(*j)))], n_first


def _stream_block(x_refs, n_first):
    x = x_refs[0][...]
    if len(x_refs) == 2:
        x = jnp.where(pl.program_id(0) < n_first, x, x_refs[1][...])
    return x


def _norm_mod_kernel(*refs, n_first):
    *x_refs, g_ref, sc_ref, sh_ref, o_ref = refs
    y = _rms(_stream_block(x_refs, n_first)) * g_ref[...]
    o_ref[...] = (y * (1.0 + sc_ref[...]) + sh_ref[...]).astype(o_ref.dtype)


def _norm_mod(xs, g, sc, sh):
    tr = TM
    x_specs, n_first = _stream_specs(xs, (tr, D_MODEL), lambda: 0)
    cond = lambda i: (_cond_of_row(i * tr), 0, 0)
    return pl.pallas_call(
        functools.partial(_norm_mod_kernel, n_first=n_first),
        grid=(TOK // tr,),
        in_specs=x_specs + [pl.BlockSpec((1, D_MODEL), lambda i: (0, 0)),
                            pl.BlockSpec((None, 1, D_MODEL), cond),
                            pl.BlockSpec((None, 1, D_MODEL), cond)],
        out_specs=pl.BlockSpec((tr, D_MODEL), lambda i: (i, 0)),
        out_shape=jax.ShapeDtypeStruct((TOK, D_MODEL), BF16),
        compiler_params=_params(1),
        name="norm_mod",
    )(*xs, g, sc, sh)


def _final_norm_kernel(x_ref, g_ref, o_ref):
    o_ref[...] = _rms(x_ref[...]) * g_ref[...]


def _final_norm(x, g, row0, rows):
    tr = TM
    return pl.pallas_call(
        _final_norm_kernel,
        grid=(rows // tr,),
        in_specs=[pl.BlockSpec((tr, D_MODEL), lambda i: (row0 // tr + i, 0)),
                  pl.BlockSpec((1, D_MODEL), lambda i: (0, 0))],
        out_specs=pl.BlockSpec((tr, D_MODEL), lambda i: (i, 0)),
        out_shape=jax.ShapeDtypeStruct((rows, D_MODEL), F32),
        compiler_params=_params(1),
        name="final_norm",
    )(x, g)


def _mm_residual_kernel(*refs, n_first):
    a_ref, w_ref, *x_refs, gt_lo_ref, gt_hi_ref, o_ref = refs
    x = _stream_block(x_refs, n_first)
    acc = _bdot(a_ref[...], w_ref[...])
    half = acc.shape[0] // 2
    o_ref[:half, :] = x[:half] + gt_lo_ref[...] * acc[:half]
    o_ref[half:, :] = x[half:] + gt_hi_ref[...] * acc[half:]


def _matmul_residual(a, w, l, xs, gt, tn, tm, name):
    m, k = a.shape
    n = w.shape[2]
    x_specs, n_first = _stream_specs(xs, (tm, tn), lambda j: j)
    return pl.pallas_call(
        functools.partial(_mm_residual_kernel, n_first=n_first),
        grid=(m // tm, n // tn),
        in_specs=[pl.BlockSpec((tm, k), lambda i, j: (i, 0)),
                  pl.BlockSpec((None, k, tn), lambda i, j: (l, 0, j))] + x_specs + [
                  pl.BlockSpec((None, 1, tn), lambda i, j: (_cond_of_row(i * tm), 0, j)),
                  pl.BlockSpec((None, 1, tn), lambda i, j: (_cond_of_row(i * tm + tm // 2), 0, j))],
        out_specs=pl.BlockSpec((tm, tn), lambda i, j: (i, j)),
        out_shape=jax.ShapeDtypeStruct((m, n), F32),
        compiler_params=_params(2),
        name=name,
    )(a, w, *xs, gt, gt)


def _tail_kernel(h_ref, wt_ref, o_ref):
    o_ref[...] = _bdot_t(h_ref[...], wt_ref[...]).astype(o_ref.dtype)


def _tail_proj(h, w_in_t, l):
    tn = 512
    in_w = w_in_t.shape[1]
    return pl.pallas_call(
        _tail_kernel,
        grid=(TOK // TMW, TAIL_W // tn),
        in_specs=[pl.BlockSpec((TMW, D_MODEL), lambda i, j: (i, 0)),
                  pl.BlockSpec((pl.Element(tn), pl.Element(D_MODEL)),
                               lambda i, j: (pl.multiple_of(l * in_w + IN_GLU + j * tn, SUBLANES), 0))],
        out_specs=pl.BlockSpec((TMW, tn), lambda i, j: (i, j)),
        out_shape=jax.ShapeDtypeStruct((TOK, TAIL_W), BF16),
        compiler_params=_params(2),
        name="tail_proj",
    )(h, w_in_t.reshape(DEPTH * in_w, D_MODEL))


def _mixer_a_kernel(h_ref, wh_ref, wb_ref, wc_ref, cw_ref, o_ref):
    is_ctx = pl.program_id(0) < CTX_TOK // TMW
    h = h_ref[...]
    p = _bdot_t(h, wc_ref[...]) * _bdot_t(h, wh_ref[...])
    o_ref[...] = (_bdot_t(h, wb_ref[...]) * _conv3_rows(p, cw_ref[...], is_ctx)).astype(o_ref.dtype)


def _mixer_a(h, w_in, conv_a, l):
    tn = 256
    col = lambda off: (lambda i, j: (l, off // tn + j, 0))
    wspec = lambda off: pl.BlockSpec((None, tn, D_MODEL), col(off))
    return pl.pallas_call(
        _mixer_a_kernel,
        grid=(TOK // TMW, A_W // tn),
        in_specs=[pl.BlockSpec((TMW, D_MODEL), lambda i, j: (i, 0)),
                  wspec(IN_AH), wspec(IN_AB), wspec(IN_AC),
                  pl.BlockSpec((None, A_KERNEL, tn), lambda i, j: (l, 0, j))],
        out_specs=pl.BlockSpec((TMW, tn), lambda i, j: (i, j)),
        out_shape=jax.ShapeDtypeStruct((TOK, A_W), BF16),
        compiler_params=_params(2),
        name="mixer_a",
    )(h, w_in, w_in, w_in, conv_a)


def _q_kernel(h_ref, wa_ref, g_ref, w_ref, o_ref, qn_ref):
    @pl.when(pl.program_id(1) == 0)
    def _():
        qn_ref[...] = (_rms(_bdot_t(h_ref[...], wa_ref[...])) * g_ref[...]).astype(BF16)

    o_ref[...] = _bdot(qn_ref[...], w_ref[...]).astype(o_ref.dtype)


def _q_proj(h, w_in, g_q, w_q, l):
    tn = 1024
    n = w_q.shape[2]
    return pl.pallas_call(
        _q_kernel,
        grid=(TOK // TMW, n // tn),
        in_specs=[pl.BlockSpec((TMW, D_MODEL), lambda i, j: (i, 0)),
                  pl.BlockSpec((None, Q_RANK, D_MODEL), lambda i, j: (l, IN_Q // Q_RANK, 0),
                               pipeline_mode=pl.Buffered(1)),
                  pl.BlockSpec((1, Q_RANK), lambda i, j: (0, 0)),
                  pl.BlockSpec((None, Q_RANK, tn), lambda i, j: (l, 0, j))],
        out_specs=pl.BlockSpec((TMW, tn), lambda i, j: (i, j)),
        out_shape=jax.ShapeDtypeStruct((TOK, n), BF16),
        scratch_shapes=[pltpu.VMEM((TMW, Q_RANK), BF16)],
        compiler_params=_params(2),
        name="q_proj",
    )(h, w_in, g_q, w_q)


def _kv_kernel(h_ref, wa_ref, g_ref, w_ref, ckv_ref, kpe_ref, kv_ref, cn_ref):
    @pl.when(pl.program_id(1) == 0)
    def _():
        kv_a = _bdot_t(h_ref[...], wa_ref[...])
        y = _rms(kv_a[:, :KV_RANK]) * g_ref[...]
        kpe_ref[...] = kv_a[:, KV_RANK:]
        cn_ref[...] = y.astype(BF16)

        @pl.when(pl.program_id(0) < CTX_TOK // TMW)
        def _():
            ckv_ref[...] = y

    kv_ref[...] = _bdot(cn_ref[...], w_ref[...]).astype(kv_ref.dtype)


def _kv_proj(h, w_in, g_kv, w_ukv, l):
    tn = 512
    n = w_ukv.shape[2]
    last_ctx = CTX_TOK // TMW - 1
    return pl.pallas_call(
        _kv_kernel,
        grid=(TOK // TMW, n // tn),
        in_specs=[pl.BlockSpec((TMW, D_MODEL), lambda i, j: (i, 0)),
                  pl.BlockSpec((None, KV_A_W, D_MODEL), lambda i, j: (l, IN_KV // KV_A_W, 0),
                               pipeline_mode=pl.Buffered(1)),
                  pl.BlockSpec((1, KV_RANK), lambda i, j: (0, 0)),
                  pl.BlockSpec((None, KV_RANK, tn), lambda i, j: (l, 0, j))],
        out_specs=[pl.BlockSpec((TMW, KV_RANK), lambda i, j: (jnp.minimum(i, last_ctx), 0)),
                   pl.BlockSpec((TMW, LANES), lambda i, j: (i, 0)),
                   pl.BlockSpec((TMW, tn), lambda i, j: (i, j))],
        out_shape=[jax.ShapeDtypeStruct((CTX_TOK, KV_RANK), F32),
                   jax.ShapeDtypeStruct((TOK, LANES), F32),
                   jax.ShapeDtypeStruct((TOK, n), BF16)],
        scratch_shapes=[pltpu.VMEM((TMW, KV_RANK), BF16)],
        compiler_params=_params(2),
        name="kv_proj",
    )(h, w_in, g_kv, w_ukv)


def _cache_kv_kernel(a_ref, w_ref, o_ref):
    o_ref[...] = _bdot(a_ref[...].astype(BF16), w_ref[...]).astype(o_ref.dtype)


def _cache_kv_proj(cache_ckv, w_ukv, l):
    tn = 1024
    n = w_ukv.shape[2]
    return pl.pallas_call(
        _cache_kv_kernel,
        grid=(DEC_BATCH, n // tn),
        in_specs=[pl.BlockSpec((None, None, PAST_LEN, KV_RANK), lambda b, j: (b, l, 0, 0)),
                  pl.BlockSpec((None, KV_RANK, tn), lambda b, j: (l, 0, j))],
        out_specs=pl.BlockSpec((PAST_LEN, tn), lambda b, j: (b, j)),
        out_shape=jax.ShapeDtypeStruct((DEC_BATCH * PAST_LEN, n), BF16),
        compiler_params=_params(2),
        name="cache_kv_proj",
    )(cache_ckv, w_ukv)


def _merge_kernel(a1_ref, w1_ref, a2_ref, w2_ref, a3_ref, w3_ref, g1_ref, g2_ref, g3_ref, o_ref):
    def branch(a_ref, w_ref, g_ref):
        return _sigmoid(g_ref[...].astype(F32)) * _bdot(a_ref[...], w_ref[...])

    o_ref[...] = (branch(a1_ref, w1_ref, g1_ref) + branch(a2_ref, w2_ref, g2_ref)
                  + branch(a3_ref, w3_ref, g3_ref)).astype(o_ref.dtype)


def _merge(ya, w_a_out, attn, w_mla_out, yc, w_c_out, tail, l):
    tn = 256
    gate = lambda k: (lambda i, j: (i, (TAIL_GATE + k * D_MODEL) // tn + j))
    act = lambda width: pl.BlockSpec((TM, width), lambda i, j: (i, 0))
    wgt = lambda width: pl.BlockSpec((None, width, tn), lambda i, j: (l, 0, j))
    return pl.pallas_call(
        _merge_kernel,
        grid=(N_ROW_BLOCKS, D_MODEL // tn),
        in_specs=[act(A_W), wgt(A_W), act(N_HEADS * V_DIM), wgt(N_HEADS * V_DIM), act(C_W), wgt(C_W),
                  pl.BlockSpec((TM, tn), gate(0)), pl.BlockSpec((TM, tn), gate(1)),
                  pl.BlockSpec((TM, tn), gate(2))],
        out_specs=pl.BlockSpec((TM, tn), lambda i, j: (i, j)),
        out_shape=jax.ShapeDtypeStruct((TOK, D_MODEL), BF16),
        compiler_params=_params(2),
        name="merge",
    )(ya, w_a_out, attn, w_mla_out, yc, w_c_out, tail, tail, tail)


FFN_TN = 512
FFN_COL_TILES = D_FF // FFN_TN


def _ffn_up_kernel(h_ref, wa_ref, wg_ref, ca_ref, cg_ref, o_ref):
    is_ctx = pl.program_id(0) < CTX_BLOCKS
    h = h_ref[...]
    ua = _bdot(h, wa_ref[...])
    ug = _bdot(h, wg_ref[...])
    a = _conv3_rows(ua, ca_ref[...], is_ctx)
    g = _conv3_rows(ug, cg_ref[...], is_ctx)
    o_ref[...] = (_silu(a) * g).astype(o_ref.dtype)


def _ffn_up(h, w_up, conv_f, l):
    tn = FFN_TN
    half = FFN_COL_TILES
    return pl.pallas_call(
        _ffn_up_kernel,
        grid=(N_ROW_BLOCKS, half),
        in_specs=[pl.BlockSpec((TM, D_MODEL), lambda i, j: (i, 0)),
                  pl.BlockSpec((None, D_MODEL, tn), lambda i, j: (l, 0, j)),
                  pl.BlockSpec((None, D_MODEL, tn), lambda i, j: (l, 0, half + j)),
                  pl.BlockSpec((None, F_KERNEL, tn), lambda i, j: (l, 0, j)),
                  pl.BlockSpec((None, F_KERNEL, tn), lambda i, j: (l, 0, half + j))],
        out_specs=pl.BlockSpec((TM, tn), lambda i, j: (i, j)),
        out_shape=jax.ShapeDtypeStruct((TOK, D_FF), BF16),
        compiler_params=_params(2),
        name="ffn_up",
    )(h, w_up, w_up, conv_f, conv_f)


def _conv_c_kernel(g1_ref, g2_ref, w_ref, lng_ref, lnb_ref, o_ref, u_ref, sh_ref, cv_ref):
    is_dec = pl.program_id(0) >= CTX_BLOCKS
    halo_w = SEQ + 2 * CONV_PAD
    n_pieces = TM // SEQ

    def gated(r0, n):
        return g1_ref[r0:r0 + n, :].astype(F32) * _sigmoid(g2_ref[r0:r0 + n, :].astype(F32))

    zeros = jnp.zeros((CONV_PAD, C_W), F32)
    u_ref[halo_w:halo_w + SUBLANES, :] = zeros[:SUBLANES]

    for p in range(n_pieces):
        r0 = p * SEQ
        top = jnp.where(is_dec, gated(r0 - CONV_PAD, CONV_PAD), 0.0) if p > 0 else zeros
        bot = jnp.where(is_dec, gated(r0 + SEQ, CONV_PAD), 0.0) if p < n_pieces - 1 else zeros
        u_ref[0:CONV_PAD, :] = top
        u_ref[CONV_PAD:CONV_PAD + SEQ, :] = gated(r0, SEQ)
        u_ref[CONV_PAD + SEQ:halo_w, :] = bot

        def shift_rows(c, carry):
            base = pl.multiple_of(c * SUBLANES, SUBLANES)
            win = u_ref[pl.ds(base, 2 * SUBLANES), :]
            sh_ref[0, pl.ds(base, SUBLANES), :] = win[:SUBLANES]
            for b in range(1, SUBLANES):
                sh_ref[b, pl.ds(base, SUBLANES), :] = pltpu.roll(win, 2 * SUBLANES - b, 0)[:SUBLANES]
            return carry

        lax.fori_loop(0, halo_w // SUBLANES, shift_rows, 0)

        for s in range(C_W // LANES):
            cols = slice(s * LANES, (s + 1) * LANES)
            taps = [jnp.broadcast_to(w_ref[k:k + 1, cols], (SUBLANES, LANES)) for k in range(C_KERNEL)]

            def conv_rows(c, carry, cols=cols, taps=taps):
                base = pl.multiple_of(c * CONV_ROWS, CONV_ROWS)
                for r in range(0, CONV_ROWS, SUBLANES):
                    acc = None
                    for k in range(C_KERNEL):
                        off = CONV_PAD - C_KERNEL // 2 + k
                        start = pl.multiple_of(base + r + (off // SUBLANES) * SUBLANES, SUBLANES)
                        term = taps[k] * sh_ref[off % SUBLANES, pl.ds(start, SUBLANES), cols]
                        acc = term if acc is None else acc + term
                    cv_ref[pl.ds(base + r, SUBLANES), cols] = acc
                return carry

            lax.fori_loop(0, SEQ // CONV_ROWS, conv_rows, 0)

        for base in range(0, SEQ, NORM_ROWS):
            v = cv_ref[base:base + NORM_ROWS, :]
            d = v - jnp.mean(v, axis=-1, keepdims=True)
            var = jnp.mean(d * d, axis=-1, keepdims=True)
            y = d * lax.rsqrt(var + EPS) * lng_ref[...] + lnb_ref[...]
            o_ref[r0 + base:r0 + base + NORM_ROWS, :] = _silu(y).astype(o_ref.dtype)


def _conv_c(tail, conv_c, ln_g, ln_b, l):
    halo_w = SEQ + 2 * CONV_PAD
    return pl.pallas_call(
        _conv_c_kernel,
        grid=(N_ROW_BLOCKS,),
        in_specs=[pl.BlockSpec((TM, C_W), lambda i: (i, TAIL_G1 // C_W)),
                  pl.BlockSpec((TM, C_W), lambda i: (i, TAIL_G2 // C_W)),
                  pl.BlockSpec((None, C_KERNEL, C_W), lambda i: (l, 0, 0)),
                  pl.BlockSpec((1, C_W), lambda i: (0, 0)),
                  pl.BlockSpec((1, C_W), lambda i: (0, 0))],
        out_specs=pl.BlockSpec((TM, C_W), lambda i: (i, 0)),
        out_shape=jax.ShapeDtypeStruct((TOK, C_W), BF16),
        scratch_shapes=[pltpu.VMEM((halo_w + SUBLANES, C_W), F32),
                        pltpu.VMEM((SUBLANES, halo_w, C_W), F32),
                        pltpu.VMEM((SEQ, C_W), F32)],
        compiler_params=_params(1),
        name="conv_c",
    )(tail, tail, conv_c, ln_g, ln_b)


def _rope_pairs(x, cos, sin):
    half = ROPE_DIM // 4
    lane = lax.broadcasted_iota(jnp.int32, x.shape, 1)
    swapped = jnp.where((lane & (2 * half - 1)) < half,
                        pltpu.roll(x, LANES - half, 1), pltpu.roll(x, half, 1))
    return x * cos + swapped * sin


def _pe_keys(k):
    lane = lax.broadcasted_iota(jnp.int32, k.shape, 1)
    lo = jnp.where(lane < ROPE_DIM, k, 0.0)
    return lo.astype(BF16), pltpu.roll(lo, ROPE_DIM, 1).astype(BF16)


def _attend(q_ref, segments, o_ref, rope_q=None):
    c = QK_DIM ** -0.5 * math.log2(math.e)
    pe_col0 = N_HEADS * NOPE_DIM
    for hp in range(HEAD_PAIRS):
        qp = q_ref[:, pe_col0 + hp * LANES:pe_col0 + (hp + 1) * LANES]
        if rope_q is not None:
            qp = _rope_pairs(qp.astype(F32), *rope_q).astype(BF16)
        for h in range(2):
            head = 2 * hp + h
            q = jnp.concatenate([q_ref[:, head * NOPE_DIM:(head + 1) * NOPE_DIM], qp], axis=1)
            kbase = head * (NOPE_DIM + V_DIM)
            scores = []
            for kv_ref, pe in segments:
                k = jnp.concatenate([kv_ref[:, kbase:kbase + NOPE_DIM], pe[h]], axis=1)
                scores.append(lax.dot_general(q, k, (((1,), (1,)), ((), ())),
                                              preferred_element_type=F32))
            m = scores[0].max(axis=-1, keepdims=True)
            for s in scores[1:]:
                m = jnp.maximum(m, s.max(axis=-1, keepdims=True))
            denom = 0.0
            out = 0.0
            for s, (kv_ref, _) in zip(scores, segments):
                p = jnp.exp2((s - m) * c)
                denom = denom + p.sum(axis=-1, keepdims=True)
                out = out + jnp.dot(p.astype(BF16), kv_ref[:, kbase + NOPE_DIM:kbase + NOPE_DIM + V_DIM],
                                    preferred_element_type=F32)
            o_ref[:, head * V_DIM:(head + 1) * V_DIM] = (out / denom).astype(o_ref.dtype)


def _attn_kernel(q_ref, kvx_ref, kpex_ref, kvd_ref, kped_ref, kvc_ref, kpec_ref,
                 cq_ref, sq_ref, ck_ref, sk_ref, o_ref):
    step = pl.program_id(0)

    @pl.when(step < BATCH)
    def _():
        _attend(q_ref, [(kvx_ref, _pe_keys(kpex_ref[...]))], o_ref)

    @pl.when(step >= BATCH)
    def _():
        new_pe = _rope_pairs(kped_ref[...], ck_ref[...], sk_ref[...])
        _attend(q_ref, [(kvc_ref, _pe_keys(kpec_ref[...])), (kvd_ref, _pe_keys(new_pe))], o_ref,
                rope_q=(cq_ref[...], sq_ref[...]))


def _attention(q, kv, kpe, kv_cache, kpe_cache, cos, sin):
    per_seq = DEC_SEQ // TQ
    n_steps = BATCH + DEC_BATCH * per_seq
    ctx = lambda s: jnp.minimum(s, BATCH - 1)
    dec = lambda s: jnp.maximum(s - BATCH, 0)
    kv_w = N_HEADS * (NOPE_DIM + V_DIM)
    in_specs = [
        pl.BlockSpec((TQ, q.shape[1]), lambda s: (s, 0)),
        pl.BlockSpec((SEQ, kv_w), lambda s: (ctx(s), 0)),
        pl.BlockSpec((SEQ, LANES), lambda s: (ctx(s), 0)),
        pl.BlockSpec((DEC_SEQ, kv_w), lambda s: (CTX_TOK // DEC_SEQ + dec(s) // per_seq, 0)),
        pl.BlockSpec((DEC_SEQ, LANES), lambda s: (CTX_TOK // DEC_SEQ + dec(s) // per_seq, 0)),
        pl.BlockSpec((PAST_LEN, kv_w), lambda s: (dec(s) // per_seq, 0)),
        pl.BlockSpec((None, PAST_LEN, LANES), lambda s: (dec(s) // per_seq, 0, 0)),
        pl.BlockSpec((TQ, LANES), lambda s: (dec(s) % per_seq, 0)),
        pl.BlockSpec((TQ, LANES), lambda s: (dec(s) % per_seq, 0)),
        pl.BlockSpec((DEC_SEQ, LANES), lambda s: (0, 0)),
        pl.BlockSpec((DEC_SEQ, LANES), lambda s: (0, 0)),
    ]
    return pl.pallas_call(
        _attn_kernel,
        grid=(n_steps,),
        in_specs=in_specs,
        out_specs=pl.BlockSpec((TQ, N_HEADS * V_DIM), lambda s: (s, 0)),
        out_shape=jax.ShapeDtypeStruct((TOK, N_HEADS * V_DIM), BF16),
        compiler_params=_params(1),
        name="attention",
    )(q, kv, kpe, kv, kpe, kv_cache, kpe_cache, cos, sin, cos, sin)


def _rope_tables():
    rows = DEC_SEQ // GRID_W
    r = jnp.repeat(jnp.arange(rows), GRID_W).astype(F32)
    col = jnp.tile(jnp.arange(GRID_W), rows).astype(F32)
    half_axis = ROPE_DIM // 2
    inv = ROPE_THETA ** (-jnp.arange(0, half_axis, 2, dtype=F32) / half_axis)
    ar = r[:, None] * inv
    ac = col[:, None] * inv
    cos = jnp.concatenate([jnp.cos(ar), jnp.cos(ar), jnp.cos(ac), jnp.cos(ac)], axis=-1)
    sin = jnp.concatenate([-jnp.sin(ar), jnp.sin(ar), -jnp.sin(ac), jnp.sin(ac)], axis=-1)
    reps = LANES // ROPE_DIM
    return jnp.tile(cos, (1, reps)), jnp.tile(sin, (1, reps))


def kernel(x_prompt, x_sample, cache_ckv, cache_kpe, c, c_ctx, w_mod, b_mod, g_norm1, g_norm2, w_in, conv_a, w_a_out, g_q, w_uq, g_kv, w_ukv, w_mla_out, conv_c, ln_c_g, ln_c_b, w_c_out, w_o, w_up, conv_f, w_down, g_final):
    xs = (x_prompt.reshape(CTX_TOK, D_MODEL), x_sample.reshape(DEC_TOK, D_MODEL))
    cond = jnp.concatenate([c_ctx[None, :], c, jnp.zeros((COND_ROWS - N_COND, D_MODEL), F32)], axis=0)
    mod = _modulation(cond, w_mod, b_mod)
    cos, sin = _rope_tables()

    w_in_t = jnp.swapaxes(w_in, 1, 2)
    wq = w_uq.reshape(DEPTH, Q_RANK, N_HEADS, QK_DIM)
    w_q = jnp.concatenate([wq[..., :NOPE_DIM].reshape(DEPTH, Q_RANK, N_HEADS * NOPE_DIM),
                           wq[..., NOPE_DIM:].reshape(DEPTH, Q_RANK, N_HEADS * ROPE_DIM)],
                          axis=2).astype(BF16)
    kpe_cache = jnp.pad(cache_kpe, ((0, 0), (0, 0), (0, 0), (0, LANES - ROPE_DIM)))

    ckv_layers, kpe_layers = [], []
    for l in range(DEPTH):
        sh1, sc1, gt1, sh2, sc2, gt2 = [
            mod[l, :N_COND, k * D_MODEL:(k + 1) * D_MODEL].reshape(N_COND, 1, D_MODEL) for k in range(6)]

        h = _norm_mod(xs, g_norm1[l][None, :], sc1, sh1)
        ya = _mixer_a(h, w_in_t, conv_a, l)
        q = _q_proj(h, w_in_t, g_q[l][None, :], w_q, l)
        ckv, kpe, kv = _kv_proj(h, w_in_t, g_kv[l][None, :], w_ukv, l)
        tail = _tail_proj(h, w_in_t, l)
        kv_cache = _cache_kv_proj(cache_ckv, w_ukv, l)
        attn = _attention(q, kv, kpe, kv_cache, kpe_cache[:, l], cos, sin)
        yc = _conv_c(tail, conv_c, ln_c_g[l][None, :], ln_c_b[l][None, :], l)
        merged = _merge(ya, w_a_out, attn, w_mla_out, yc, w_c_out, tail, l)
        xs = (_matmul_residual(merged, w_o, l, xs, gt1, tn=512, tm=TMW, name="out_proj"),)

        h = _norm_mod(xs, g_norm2[l][None, :], sc2, sh2)
        act = _ffn_up(h, w_up, conv_f, l)
        xs = (_matmul_residual(act, w_down, l, xs, gt2, tn=256, tm=TM, name="ffn_down"),)

        ckv_layers.append(ckv.reshape(BATCH, SEQ, KV_RANK))
        kpe_layers.append(kpe[:CTX_TOK, :ROPE_DIM].reshape(BATCH, SEQ, ROPE_DIM))

    g = g_final[None, :]
    y_prompt = _final_norm(xs[0], g, 0, CTX_TOK).reshape(BATCH, SEQ, D_MODEL)
    y_sample = _final_norm(xs[0], g, CTX_TOK, DEC_TOK).reshape(DEC_BATCH, DEC_SEQ, D_MODEL)
    return (y_prompt, y_sample, jnp.stack(ckv_layers, axis=1), jnp.stack(kpe_layers, axis=1))
```

```python
import functools
import math

import jax
import jax.numpy as jnp
from jax import lax
from jax.experimental import pallas as pl
from jax.experimental.pallas import tpu as pltpu

D_MODEL = 2048
BATCH = 16
SEQ = 256
DEPTH = 2
DEC_BATCH = 2
DEC_SEQ = 1024
PAST_LEN = 256
GRID_W = 64
A_W = 1024
A_KERNEL = 3
N_HEADS = 16
Q_RANK = 768
KV_RANK = 512
NOPE_DIM = 128
ROPE_DIM = 64
V_DIM = 128
QK_DIM = NOPE_DIM + ROPE_DIM
C_W = 1024
C_KERNEL = 31
D_FF = 5632
F_KERNEL = 3
ROPE_THETA = 10000.0
EPS = 1e-6

CTX_TOK = BATCH * SEQ
DEC_TOK = DEC_BATCH * DEC_SEQ
TOK = CTX_TOK + DEC_TOK
TM = 1024
N_ROW_BLOCKS = TOK // TM
CTX_BLOCKS = CTX_TOK // TM
TMW = 2048
N_COND = 3
COND_ROWS = 8
LANES = 128
SUBLANES = 8
HEAD_PAIRS = N_HEADS // 2
KV_A_W = 640
CONV_PAD = 16
CONV_ROWS = 64
NORM_ROWS = 64
TQ = 256
VMEM_LIMIT = 56 * 1024 * 1024

IN_AH, IN_AB, IN_AC = 0, A_W, 2 * A_W
IN_Q = 3 * A_W
IN_KV = IN_Q + Q_RANK
IN_GLU = IN_KV + KV_RANK + ROPE_DIM
TAIL_G1, TAIL_G2, TAIL_GATE = 0, C_W, 2 * C_W
TAIL_W = 2 * C_W + 3 * D_MODEL

F32 = jnp.float32
BF16 = jnp.bfloat16


def _params(n_axes):
    return pltpu.CompilerParams(dimension_semantics=("arbitrary",) * n_axes,
                                vmem_limit_bytes=VMEM_LIMIT)


def _cond_of_row(row):
    return jnp.maximum(row // DEC_SEQ - (CTX_TOK // DEC_SEQ - 1), 0)


def _sigmoid(x):
    return 0.5 * jnp.tanh(0.5 * x) + 0.5


def _silu(x):
    return x * _sigmoid(x)


def _bdot(a, w):
    return jnp.dot(a, w.astype(BF16), preferred_element_type=F32)


def _bdot_t(a, wt):
    return lax.dot_general(a, wt.astype(BF16), (((1,), (1,)), ((), ())), preferred_element_type=F32)


def _conv3_rows(u, w, is_ctx):
    n, width = u.shape
    inside = jnp.where(is_ctx, 0.0, 1.0)
    sub = lax.broadcasted_iota(jnp.int32, (SUBLANES, width), 0)
    zeros = jnp.zeros((SUBLANES, width), F32)
    pieces = []
    for r0 in range(0, n, SEQ):
        up = u[r0:r0 + SEQ]
        prev = pltpu.roll(up, 1, 0)
        nxt = pltpu.roll(up, SEQ - 1, 0)
        before = pltpu.roll(u[r0 - SUBLANES:r0], 1, 0) * inside if r0 % DEC_SEQ else zeros
        after = (pltpu.roll(u[r0 + SEQ:r0 + SEQ + SUBLANES], SUBLANES - 1, 0) * inside
                 if (r0 + SEQ) % DEC_SEQ else zeros)
        prev = jnp.concatenate([jnp.where(sub == 0, before, prev[:SUBLANES]), prev[SUBLANES:]], axis=0)
        nxt = jnp.concatenate([nxt[:SEQ - SUBLANES],
                               jnp.where(sub == SUBLANES - 1, after, nxt[SEQ - SUBLANES:])], axis=0)
        pieces.append(w[0:1, :] * prev + w[1:2, :] * up + w[2:3, :] * nxt)
    return jnp.concatenate(pieces, axis=0)


def _mod_kernel(c_ref, w_ref, b_ref, o_ref):
    s = _silu(c_ref[...]).astype(BF16)
    o_ref[...] = _bdot(s, w_ref[...]) + b_ref[...]


def _modulation(cond, w_mod, b_mod):
    tn = 1024
    n = 6 * D_MODEL
    return pl.pallas_call(
        _mod_kernel,
        grid=(DEPTH, n // tn),
        in_specs=[pl.BlockSpec((COND_ROWS, D_MODEL), lambda l, j: (0, 0)),
                  pl.BlockSpec((None, D_MODEL, tn), lambda l, j: (l, 0, j)),
                  pl.BlockSpec((None, 1, tn), lambda l, j: (l, 0, j))],
        out_specs=pl.BlockSpec((None, COND_ROWS, tn), lambda l, j: (l, 0, j)),
        out_shape=jax.ShapeDtypeStruct((DEPTH, COND_ROWS, n), F32),
        compiler_params=_params(2),
        name="modulation",
    )(cond, w_mod, b_mod.reshape(DEPTH, 1, n))


def _rms(x):
    return x * lax.rsqrt(jnp.mean(x * x, axis=-1, keepdims=True) + EPS)


def _stream_specs(xs, block, col):
    if len(xs) == 1:
        return [pl.BlockSpec(block, lambda i, *j: (i, col(*j)))], None
    n_first = xs[0].shape[0] // block[0]
    return [pl.BlockSpec(block, lambda i, *j: (jnp.minimum(i, n_first - 1),
                                               jnp.where(i < n_first, col(*j), 0))),
            pl.BlockSpec(block, lambda i, *j: (jnp.maximum(i - n_first, 0),
                                               jnp.where(i < n_first, 0, col(*j))))], n_first


def _stream_block(x_refs, n_first):
    x = x_refs[0][...]
    if len(x_refs) == 2:
        x = jnp.where(pl.program_id(0) < n_first, x, x_refs[1][...])
    return x


def _norm_mod_kernel(*refs, n_first):
    *x_refs, g_ref, sc_ref, sh_ref, o_ref = refs
    y = _rms(_stream_block(x_refs, n_first)) * g_ref[...]
    o_ref[...] = (y * (1.0 + sc_ref[...]) + sh_ref[...]).astype(o_ref.dtype)


def _norm_mod(xs, g, sc, sh):
    tr = TM
    x_specs, n_first = _stream_specs(xs, (tr, D_MODEL), lambda: 0)
    cond = lambda i: (_cond_of_row(i * tr), 0, 0)
    return pl.pallas_call(
        functools.partial(_norm_mod_kernel, n_first=n_first),
        grid=(TOK // tr,),
        in_specs=x_specs + [pl.BlockSpec((1, D_MODEL), lambda i: (0, 0)),
                            pl.BlockSpec((None, 1, D_MODEL), cond),
                            pl.BlockSpec((None, 1, D_MODEL), cond)],
        out_specs=pl.BlockSpec((tr, D_MODEL), lambda i: (i, 0)),
        out_shape=jax.ShapeDtypeStruct((TOK, D_MODEL), BF16),
        compiler_params=_params(1),
        name="norm_mod",
    )(*xs, g, sc, sh)


def _final_norm_kernel(x_ref, g_ref, o_ref):
    o_ref[...] = _rms(x_ref[...]) * g_ref[...]


def _final_norm(x, g, row0, rows):
    tr = TM
    return pl.pallas_call(
        _final_norm_kernel,
        grid=(rows // tr,),
        in_specs=[pl.BlockSpec((tr, D_MODEL), lambda i: (row0 // tr + i, 0)),
                  pl.BlockSpec((1, D_MODEL), lambda i: (0, 0))],
        out_specs=pl.BlockSpec((tr, D_MODEL), lambda i: (i, 0)),
        out_shape=jax.ShapeDtypeStruct((rows, D_MODEL), F32),
        compiler_params=_params(1),
        name="final_norm",
    )(x, g)


def _mm_residual_kernel(*refs, n_first):
    a_ref, w_ref, *x_refs, gt_lo_ref, gt_hi_ref, o_ref = refs
    x = _stream_block(x_refs, n_first)
    acc = _bdot(a_ref[...], w_ref[...])
    half = acc.shape[0] // 2
    o_ref[:half, :] = x[:half] + gt_lo_ref[...] * acc[:half]
    o_ref[half:, :] = x[half:] + gt_hi_ref[...] * acc[half:]


def _matmul_residual(a, w, l, xs, gt, tn, tm, name):
    m, k = a.shape
    n = w.shape[2]
    x_specs, n_first = _stream_specs(xs, (tm, tn), lambda j: j)
    return pl.pallas_call(
        functools.partial(_mm_residual_kernel, n_first=n_first),
        grid=(m // tm, n // tn),
        in_specs=[pl.BlockSpec((tm, k), lambda i, j: (i, 0)),
                  pl.BlockSpec((None, k, tn), lambda i, j: (l, 0, j))] + x_specs + [
                  pl.BlockSpec((None, 1, tn), lambda i, j: (_cond_of_row(i * tm), 0, j)),
                  pl.BlockSpec((None, 1, tn), lambda i, j: (_cond_of_row(i * tm + tm // 2), 0, j))],
        out_specs=pl.BlockSpec((tm, tn), lambda i, j: (i, j)),
        out_shape=jax.ShapeDtypeStruct((m, n), F32),
        compiler_params=_params(2),
        name=name,
    )(a, w, *xs, gt, gt)


def _tail_kernel(h_ref, wt_ref, o_ref):
    o_ref[...] = _bdot_t(h_ref[...], wt_ref[...]).astype(o_ref.dtype)


def _tail_proj(h, w_in_t, l):
    tn = 512
    in_w = w_in_t.shape[1]
    return pl.pallas_call(
        _tail_kernel,
        grid=(TOK // TMW, TAIL_W // tn),
        in_specs=[pl.BlockSpec((TMW, D_MODEL), lambda i, j: (i, 0)),
                  pl.BlockSpec((pl.Element(tn), pl.Element(D_MODEL)),
                               lambda i, j: (pl.multiple_of(l * in_w + IN_GLU + j * tn, SUBLANES), 0))],
        out_specs=pl.BlockSpec((TMW, tn), lambda i, j: (i, j)),
        out_shape=jax.ShapeDtypeStruct((TOK, TAIL_W), BF16),
        compiler_params=_params(2),
        name="tail_proj",
    )(h, w_in_t.reshape(DEPTH * in_w, D_MODEL))


def _mixer_a_kernel(h_ref, wh_ref, wb_ref, wc_ref, cw_ref, o_ref):
    is_ctx = pl.program_id(0) < CTX_TOK // TMW
    h = h_ref[...]
    p = _bdot_t(h, wc_ref[...]) * _bdot_t(h, wh_ref[...])
    o_ref[...] = (_bdot_t(h, wb_ref[...]) * _conv3_rows(p, cw_ref[...], is_ctx)).astype(o_ref.dtype)


def _mixer_a(h, w_in, conv_a, l):
    tn = 256
    col = lambda off: (lambda i, j: (l, off // tn + j, 0))
    wspec = lambda off: pl.BlockSpec((None, tn, D_MODEL), col(off))
    return pl.pallas_call(
        _mixer_a_kernel,
        grid=(TOK // TMW, A_W // tn),
        in_specs=[pl.BlockSpec((TMW, D_MODEL), lambda i, j: (i, 0)),
                  wspec(IN_AH), wspec(IN_AB), wspec(IN_AC),
                  pl.BlockSpec((None, A_KERNEL, tn), lambda i, j: (l, 0, j))],
        out_specs=pl.BlockSpec((TMW, tn), lambda i, j: (i, j)),
        out_shape=jax.ShapeDtypeStruct((TOK, A_W), BF16),
        compiler_params=_params(2),
        name="mixer_a",
    )(h, w_in, w_in, w_in, conv_a)


def _q_kernel(h_ref, wa_ref, g_ref, w_ref, o_ref, qn_ref):
    @pl.when(pl.program_id(1) == 0)
    def _():
        qn_ref[...] = (_rms(_bdot_t(h_ref[...], wa_ref[...])) * g_ref[...]).astype(BF16)

    o_ref[...] = _bdot(qn_ref[...], w_ref[...]).astype(o_ref.dtype)


def _q_proj(h, w_in, g_q, w_q, l):
    tn = 1024
    n = w_q.shape[2]
    return pl.pallas_call(
        _q_kernel,
        grid=(TOK // TMW, n // tn),
        in_specs=[pl.BlockSpec((TMW, D_MODEL), lambda i, j: (i, 0)),
                  pl.BlockSpec((None, Q_RANK, D_MODEL), lambda i, j: (l, IN_Q // Q_RANK, 0),
                               pipeline_mode=pl.Buffered(1)),
                  pl.BlockSpec((1, Q_RANK), lambda i, j: (0, 0)),
                  pl.BlockSpec((None, Q_RANK, tn), lambda i, j: (l, 0, j))],
        out_specs=pl.BlockSpec((TMW, tn), lambda i, j: (i, j)),
        out_shape=jax.ShapeDtypeStruct((TOK, n), BF16),
        scratch_shapes=[pltpu.VMEM((TMW, Q_RANK), BF16)],
        compiler_params=_params(2),
        name="q_proj",
    )(h, w_in, g_q, w_q)


def _kv_kernel(h_ref, wa_ref, g_ref, w_ref, ckv_ref, kpe_ref, kv_ref, cn_ref):
    @pl.when(pl.program_id(1) == 0)
    def _():
        kv_a = _bdot_t(h_ref[...], wa_ref[...])
        y = _rms(kv_a[:, :KV_RANK]) * g_ref[...]
        kpe_ref[...] = kv_a[:, KV_RANK:]
        cn_ref[...] = y.astype(BF16)

        @pl.when(pl.program_id(0) < CTX_TOK // TMW)
        def _():
            ckv_ref[...] = y

    kv_ref[...] = _bdot(cn_ref[...], w_ref[...]).astype(kv_ref.dtype)


def _kv_proj(h, w_in, g_kv, w_ukv, l):
    tn = 512
    n = w_ukv.shape[2]
    last_ctx = CTX_TOK // TMW - 1
    return pl.pallas_call(
        _kv_kernel,
        grid=(TOK // TMW, n // tn),
        in_specs=[pl.BlockSpec((TMW, D_MODEL), lambda i, j: (i, 0)),
                  pl.BlockSpec((None, KV_A_W, D_MODEL), lambda i, j: (l, IN_KV // KV_A_W, 0),
                               pipeline_mode=pl.Buffered(1)),
                  pl.BlockSpec((1, KV_RANK), lambda i, j: (0, 0)),
                  pl.BlockSpec((None, KV_RANK, tn), lambda i, j: (l, 0, j))],
        out_specs=[pl.BlockSpec((TMW, KV_RANK), lambda i, j: (jnp.minimum(i, last_ctx), 0)),
                   pl.BlockSpec((TMW, LANES), lambda i, j: (i, 0)),
                   pl.BlockSpec((TMW, tn), lambda i, j: (i, j))],
        out_shape=[jax.ShapeDtypeStruct((CTX_TOK, KV_RANK), F32),
                   jax.ShapeDtypeStruct((TOK, LANES), F32),
                   jax.ShapeDtypeStruct((TOK, n), BF16)],
        scratch_shapes=[pltpu.VMEM((TMW, KV_RANK), BF16)],
        compiler_params=_params(2),
        name="kv_proj",
    )(h, w_in, g_kv, w_ukv)


def _cache_kv_kernel(a_ref, w_ref, o_ref):
    o_ref[...] = _bdot(a_ref[...].astype(BF16), w_ref[...]).astype(o_ref.dtype)


def _cache_kv_proj(cache_ckv, w_ukv, l):
    tn = 1024
    n = w_ukv.shape[2]
    return pl.pallas_call(
        _cache_kv_kernel,
        grid=(DEC_BATCH, n // tn),
        in_specs=[pl.BlockSpec((None, None, PAST_LEN, KV_RANK), lambda b, j: (b, l, 0, 0)),
                  pl.BlockSpec((None, KV_RANK, tn), lambda b, j: (l, 0, j))],
        out_specs=pl.BlockSpec((PAST_LEN, tn), lambda b, j: (b, j)),
        out_shape=jax.ShapeDtypeStruct((DEC_BATCH * PAST_LEN, n), BF16),
        compiler_params=_params(2),
        name="cache_kv_proj",
    )(cache_ckv, w_ukv)


def _merge_kernel(a1_ref, w1_ref, a2_ref, w2_ref, a3_ref, w3_ref, g1_ref, g2_ref, g3_ref, o_ref):
    def branch(a_ref, w_ref, g_ref):
        return _sigmoid(g_ref[...].astype(F32)) * _bdot(a_ref[...], w_ref[...])

    o_ref[...] = (branch(a1_ref, w1_ref, g1_ref) + branch(a2_ref, w2_ref, g2_ref)
                  + branch(a3_ref, w3_ref, g3_ref)).astype(o_ref.dtype)


def _merge(ya, w_a_out, attn, w_mla_out, yc, w_c_out, tail, l):
    tn = 256
    gate = lambda k: (lambda i, j: (i, (TAIL_GATE + k * D_MODEL) // tn + j))
    act = lambda width: pl.BlockSpec((TM, width), lambda i, j: (i, 0))
    wgt = lambda width: pl.BlockSpec((None, width, tn), lambda i, j: (l, 0, j))
    return pl.pallas_call(
        _merge_kernel,
        grid=(N_ROW_BLOCKS, D_MODEL // tn),
        in_specs=[act(A_W), wgt(A_W), act(N_HEADS * V_DIM), wgt(N_HEADS * V_DIM), act(C_W), wgt(C_W),
                  pl.BlockSpec((TM, tn), gate(0)), pl.BlockSpec((TM, tn), gate(1)),
                  pl.BlockSpec((TM, tn), gate(2))],
        out_specs=pl.BlockSpec((TM, tn), lambda i, j: (i, j)),
        out_shape=jax.ShapeDtypeStruct((TOK, D_MODEL), BF16),
        compiler_params=_params(2),
        name="merge",
    )(ya, w_a_out, attn, w_mla_out, yc, w_c_out, tail, tail, tail)


FFN_TN = 512
FFN_COL_TILES = D_FF // FFN_TN


def _ffn_up_kernel(h_ref, wa_ref, wg_ref, ca_ref, cg_ref, o_ref):
    is_ctx = pl.program_id(0) < CTX_BLOCKS
    h = h_ref[...]
    ua = _bdot(h, wa_ref[...])
    ug = _bdot(h, wg_ref[...])
    a = _conv3_rows(ua, ca_ref[...], is_ctx)
    g = _conv3_rows(ug, cg_ref[...], is_ctx)
    o_ref[...] = (_silu(a) * g).astype(o_ref.dtype)


def _ffn_up(h, w_up, conv_f, l):
    tn = FFN_TN
    half = FFN_COL_TILES
    return pl.pallas_call(
        _ffn_up_kernel,
        grid=(N_ROW_BLOCKS, half),
        in_specs=[pl.BlockSpec((TM, D_MODEL), lambda i, j: (i, 0)),
                  pl.BlockSpec((None, D_MODEL, tn), lambda i, j: (l, 0, j)),
                  pl.BlockSpec((None, D_MODEL, tn), lambda i, j: (l, 0, half + j)),
                  pl.BlockSpec((None, F_KERNEL, tn), lambda i, j: (l, 0, j)),
                  pl.BlockSpec((None, F_KERNEL, tn), lambda i, j: (l, 0, half + j))],
        out_specs=pl.BlockSpec((TM, tn), lambda i, j: (i, j)),
        out_shape=jax.ShapeDtypeStruct((TOK, D_FF), BF16),
        compiler_params=_params(2),
        name="ffn_up",
    )(h, w_up, w_up, conv_f, conv_f)


def _conv_c_kernel(g1_ref, g2_ref, w_ref, lng_ref, lnb_ref, o_ref, u_ref, sh_ref, cv_ref):
    is_dec = pl.program_id(0) >= CTX_BLOCKS
    halo_w = SEQ + 2 * CONV_PAD
    n_pieces = TM // SEQ

    def gated(r0, n):
        return g1_ref[r0:r0 + n, :].astype(F32) * _sigmoid(g2_ref[r0:r0 + n, :].astype(F32))

    zeros = jnp.zeros((CONV_PAD, C_W), F32)
    u_ref[halo_w:halo_w + SUBLANES, :] = zeros[:SUBLANES]

    for p in range(n_pieces):
        r0 = p * SEQ
        top = jnp.where(is_dec, gated(r0 - CONV_PAD, CONV_PAD), 0.0) if p > 0 else zeros
        bot = jnp.where(is_dec, gated(r0 + SEQ, CONV_PAD), 0.0) if p < n_pieces - 1 else zeros
        u_ref[0:CONV_PAD, :] = top
        u_ref[CONV_PAD:CONV_PAD + SEQ, :] = gated(r0, SEQ)
        u_ref[CONV_PAD + SEQ:halo_w, :] = bot

        def shift_rows(c, carry):
            base = pl.multiple_of(c * SUBLANES, SUBLANES)
            win = u_ref[pl.ds(base, 2 * SUBLANES), :]
            for b in range(1, SUBLANES):
                sh_ref[b - 1, pl.ds(base, SUBLANES), :] = pltpu.roll(win, 2 * SUBLANES - b, 0)[:SUBLANES]
            return carry

        lax.fori_loop(0, halo_w // SUBLANES, shift_rows, 0)

        for s in range(C_W // LANES):
            cols = slice(s * LANES, (s + 1) * LANES)
            taps = [jnp.broadcast_to(w_ref[k:k + 1, cols], (SUBLANES, LANES)) for k in range(C_KERNEL)]

            def conv_rows(c, carry, cols=cols, taps=taps):
                base = pl.multiple_of(c * CONV_ROWS, CONV_ROWS)
                for r in range(0, CONV_ROWS, SUBLANES):
                    acc = None
                    for k in range(C_KERNEL):
                        off = CONV_PAD - C_KERNEL // 2 + k
                        start = pl.multiple_of(base + r + (off // SUBLANES) * SUBLANES, SUBLANES)
                        b = off % SUBLANES
                        shifted = sh_ref.at[b - 1] if b else u_ref
                        term = taps[k] * shifted[pl.ds(start, SUBLANES), cols]
                        acc = term if acc is None else acc + term
                    cv_ref[pl.ds(base + r, SUBLANES), cols] = acc
                return carry

            lax.fori_loop(0, SEQ // CONV_ROWS, conv_rows, 0)

        for base in range(0, SEQ, NORM_ROWS):
            v = cv_ref[base:base + NORM_ROWS, :]
            d = v - jnp.mean(v, axis=-1, keepdims=True)
            var = jnp.mean(d * d, axis=-1, keepdims=True)
            y = d * lax.rsqrt(var + EPS) * lng_ref[...] + lnb_ref[...]
            o_ref[r0 + base:r0 + base + NORM_ROWS, :] = _silu(y).astype(o_ref.dtype)


def _conv_c(tail, conv_c, ln_g, ln_b, l):
    halo_w = SEQ + 2 * CONV_PAD
    return pl.pallas_call(
        _conv_c_kernel,
        grid=(N_ROW_BLOCKS,),
        in_specs=[pl.BlockSpec((TM, C_W), lambda i: (i, TAIL_G1 // C_W)),
                  pl.BlockSpec((TM, C_W), lambda i: (i, TAIL_G2 // C_W)),
                  pl.BlockSpec((None, C_KERNEL, C_W), lambda i: (l, 0, 0)),
                  pl.BlockSpec((1, C_W), lambda i: (0, 0)),
                  pl.BlockSpec((1, C_W), lambda i: (0, 0))],
        out_specs=pl.BlockSpec((TM, C_W), lambda i: (i, 0)),
        out_shape=jax.ShapeDtypeStruct((TOK, C_W), BF16),
        scratch_shapes=[pltpu.VMEM((halo_w + SUBLANES, C_W), F32),
                        pltpu.VMEM((SUBLANES - 1, halo_w, C_W), F32),
                        pltpu.VMEM((SEQ, C_W), F32)],
        compiler_params=_params(1),
        name="conv_c",
    )(tail, tail, conv_c, ln_g, ln_b)


def _rope_pairs(x, cos, sin):
    half = ROPE_DIM // 4
    lane = lax.broadcasted_iota(jnp.int32, x.shape, 1)
    swapped = jnp.where((lane & (2 * half - 1)) < half,
                        pltpu.roll(x, LANES - half, 1), pltpu.roll(x, half, 1))
    return x * cos + swapped * sin


def _pe_keys(k):
    lane = lax.broadcasted_iota(jnp.int32, k.shape, 1)
    lo = jnp.where(lane < ROPE_DIM, k, 0.0)
    return lo.astype(BF16), pltpu.roll(lo, ROPE_DIM, 1).astype(BF16)


def _attend(q_ref, rows, segments, o_ref, rope_q=None):
    c = QK_DIM ** -0.5 * math.log2(math.e)
    pe_col0 = N_HEADS * NOPE_DIM
    ones = [jnp.ones((pe[0].shape[0], V_DIM), BF16) for _, _, pe in segments]
    for hp in range(HEAD_PAIRS):
        qp = q_ref[rows, pe_col0 + hp * LANES:pe_col0 + (hp + 1) * LANES]
        if rope_q is not None:
            qp = _rope_pairs(qp.astype(F32), *rope_q).astype(BF16)
        for h in range(2):
            head = 2 * hp + h
            q = jnp.concatenate([q_ref[rows, head * NOPE_DIM:(head + 1) * NOPE_DIM], qp], axis=1)
            kbase = head * (NOPE_DIM + V_DIM)
            scores = []
            for kv_ref, krows, pe in segments:
                k = jnp.concatenate([kv_ref[krows, kbase:kbase + NOPE_DIM], pe[h]], axis=1)
                scores.append(lax.dot_general(q, k, (((1,), (1,)), ((), ())),
                                              preferred_element_type=F32))
            m = scores[0].max(axis=-1, keepdims=True)
            for s in scores[1:]:
                m = jnp.maximum(m, s.max(axis=-1, keepdims=True))
            acc = 0.0
            for s, (kv_ref, krows, _), one in zip(scores, segments, ones):
                p = jnp.exp2((s - m) * c)
                v = jnp.concatenate([kv_ref[krows, kbase + NOPE_DIM:kbase + NOPE_DIM + V_DIM], one], axis=1)
                acc = acc + jnp.dot(p.astype(BF16), v, preferred_element_type=F32)
            o_ref[rows, head * V_DIM:(head + 1) * V_DIM] = (acc[:, :V_DIM] / acc[:, V_DIM:]).astype(o_ref.dtype)


def _attn_kernel(q_ref, kvx_ref, kpex_ref, kvd_ref, kped_ref, kvc_ref, kpec_ref,
                 cq_ref, sq_ref, ck_ref, sk_ref, o_ref):
    step = pl.program_id(0)
    everything = slice(None)

    @pl.when(step < CTX_TOK // TQ)
    def _():
        for r0 in range(0, TQ, SEQ):
            rows = slice(r0, r0 + SEQ)
            _attend(q_ref, rows, [(kvx_ref, rows, _pe_keys(kpex_ref[rows, :]))], o_ref)

    @pl.when(step >= CTX_TOK // TQ)
    def _():
        new_pe = _rope_pairs(kped_ref[...], ck_ref[...], sk_ref[...])
        _attend(q_ref, everything,
                [(kvc_ref, everything, _pe_keys(kpec_ref[...])), (kvd_ref, everything, _pe_keys(new_pe))],
                o_ref, rope_q=(cq_ref[...], sq_ref[...]))


def _attention(q, kv, kpe, kv_cache, kpe_cache, cos, sin):
    per_seq = DEC_SEQ // TQ
    ctx_steps = CTX_TOK // TQ
    n_steps = ctx_steps + DEC_BATCH * per_seq
    ctx = lambda s: jnp.minimum(s, ctx_steps - 1)
    dec = lambda s: jnp.maximum(s - ctx_steps, 0)
    kv_w = N_HEADS * (NOPE_DIM + V_DIM)
    in_specs = [
        pl.BlockSpec((TQ, q.shape[1]), lambda s: (s, 0)),
        pl.BlockSpec((TQ, kv_w), lambda s: (ctx(s), 0)),
        pl.BlockSpec((TQ, LANES), lambda s: (ctx(s), 0)),
        pl.BlockSpec((DEC_SEQ, kv_w), lambda s: (CTX_TOK // DEC_SEQ + dec(s) // per_seq, 0)),
        pl.BlockSpec((DEC_SEQ, LANES), lambda s: (CTX_TOK // DEC_SEQ + dec(s) // per_seq, 0)),
        pl.BlockSpec((PAST_LEN, kv_w), lambda s: (dec(s) // per_seq, 0)),
        pl.BlockSpec((None, PAST_LEN, LANES), lambda s: (dec(s) // per_seq, 0, 0)),
        pl.BlockSpec((TQ, LANES), lambda s: (dec(s) % per_seq, 0)),
        pl.BlockSpec((TQ, LANES), lambda s: (dec(s) % per_seq, 0)),
        pl.BlockSpec((DEC_SEQ, LANES), lambda s: (0, 0)),
        pl.BlockSpec((DEC_SEQ, LANES), lambda s: (0, 0)),
    ]
    return pl.pallas_call(
        _attn_kernel,
        grid=(n_steps,),
        in_specs=in_specs,
        out_specs=pl.BlockSpec((TQ, N_HEADS * V_DIM), lambda s: (s, 0)),
        out_shape=jax.ShapeDtypeStruct((TOK, N_HEADS * V_DIM), BF16),
        compiler_params=_params(1),
        name="attention",
    )(q, kv, kpe, kv, kpe, kv_cache, kpe_cache, cos, sin, cos, sin)


def _rope_tables():
    rows = DEC_SEQ // GRID_W
    r = jnp.repeat(jnp.arange(rows), GRID_W).astype(F32)
    col = jnp.tile(jnp.arange(GRID_W), rows).astype(F32)
    half_axis = ROPE_DIM // 2
    inv = ROPE_THETA ** (-jnp.arange(0, half_axis, 2, dtype=F32) / half_axis)
    ar = r[:, None] * inv
    ac = col[:, None] * inv
    cos = jnp.concatenate([jnp.cos(ar), jnp.cos(ar), jnp.cos(ac), jnp.cos(ac)], axis=-1)
    sin = jnp.concatenate([-jnp.sin(ar), jnp.sin(ar), -jnp.sin(ac), jnp.sin(ac)], axis=-1)
    reps = LANES // ROPE_DIM
    return jnp.tile(cos, (1, reps)), jnp.tile(sin, (1, reps))


def kernel(x_prompt, x_sample, cache_ckv, cache_kpe, c, c_ctx, w_mod, b_mod, g_norm1, g_norm2, w_in, conv_a, w_a_out, g_q, w_uq, g_kv, w_ukv, w_mla_out, conv_c, ln_c_g, ln_c_b, w_c_out, w_o, w_up, conv_f, w_down, g_final):
    xs = (x_prompt.reshape(CTX_TOK, D_MODEL), x_sample.reshape(DEC_TOK, D_MODEL))
    cond = jnp.concatenate([c_ctx[None, :], c, jnp.zeros((COND_ROWS - N_COND, D_MODEL), F32)], axis=0)
    mod = _modulation(cond, w_mod, b_mod)
    cos, sin = _rope_tables()

    w_in_t = jnp.swapaxes(w_in, 1, 2)
    wq = w_uq.reshape(DEPTH, Q_RANK, N_HEADS, QK_DIM)
    w_q = jnp.concatenate([wq[..., :NOPE_DIM].reshape(DEPTH, Q_RANK, N_HEADS * NOPE_DIM),
                           wq[..., NOPE_DIM:].reshape(DEPTH, Q_RANK, N_HEADS * ROPE_DIM)],
                          axis=2).astype(BF16)
    kpe_cache = jnp.pad(cache_kpe, ((0, 0), (0, 0), (0, 0), (0, LANES - ROPE_DIM)))

    ckv_layers, kpe_layers = [], []
    for l in range(DEPTH):
        sh1, sc1, gt1, sh2, sc2, gt2 = [
            mod[l, :N_COND, k * D_MODEL:(k + 1) * D_MODEL].reshape(N_COND, 1, D_MODEL) for k in range(6)]

        h = _norm_mod(xs, g_norm1[l][None, :], sc1, sh1)
        ya = _mixer_a(h, w_in_t, conv_a, l)
        q = _q_proj(h, w_in_t, g_q[l][None, :], w_q, l)
        ckv, kpe, kv = _kv_proj(h, w_in_t, g_kv[l][None, :], w_ukv, l)
        tail = _tail_proj(h, w_in_t, l)
        kv_cache = _cache_kv_proj(cache_ckv, w_ukv, l)
        attn = _attention(q, kv, kpe, kv_cache, kpe_cache[:, l], cos, sin)
        yc = _conv_c(tail, conv_c, ln_c_g[l][None, :], ln_c_b[l][None, :], l)
        merged = _merge(ya, w_a_out, attn, w_mla_out, yc, w_c_out, tail, l)
        xs = (_matmul_residual(merged, w_o, l, xs, gt1, tn=512, tm=TMW, name="out_proj"),)

        h = _norm_mod(xs, g_norm2[l][None, :], sc2, sh2)
        act = _ffn_up(h, w_up, conv_f, l)
        xs = (_matmul_residual(act, w_down, l, xs, gt2, tn=256, tm=TM, name="ffn_down"),)

        ckv_layers.append(ckv.reshape(BATCH, SEQ, KV_RANK))
        kpe_layers.append(kpe[:CTX_TOK, :ROPE_DIM].reshape(BATCH, SEQ, ROPE_DIM))

    g = g_final[None, :]
    y_prompt = _final_norm(xs[0], g, 0, CTX_TOK).reshape(BATCH, SEQ, D_MODEL)
    y_sample = _final_norm(xs[0], g, CTX_TOK, DEC_TOK).reshape(DEC_BATCH, DEC_SEQ, D_MODEL)
    return (y_prompt, y_sample, jnp.stack(ckv_layers, axis=1), jnp.stack(kpe_layers, axis=1))
```

```python
import functools
import math

import jax
import jax.numpy as jnp
from jax import lax
from jax.experimental import pallas as pl
from jax.experimental.pallas import tpu as pltpu

D_MODEL = 2048
BATCH = 16
SEQ = 256
DEPTH = 2
DEC_BATCH = 2
DEC_SEQ = 1024
PAST_LEN = 256
GRID_W = 64
A_W = 1024
A_KERNEL = 3
N_HEADS = 16
Q_RANK = 768
KV_RANK = 512
NOPE_DIM = 128
ROPE_DIM = 64
V_DIM = 128
QK_DIM = NOPE_DIM + ROPE_DIM
C_W = 1024
C_KERNEL = 31
D_FF = 5632
F_KERNEL = 3
ROPE_THETA = 10000.0
EPS = 1e-6

CTX_TOK = BATCH * SEQ
DEC_TOK = DEC_BATCH * DEC_SEQ
TOK = CTX_TOK + DEC_TOK
TM = 1024
N_ROW_BLOCKS = TOK // TM
CTX_BLOCKS = CTX_TOK // TM
TMW = 2048
N_COND = 3
COND_ROWS = 8
LANES = 128
SUBLANES = 8
HEAD_PAIRS = N_HEADS // 2
KV_A_W = 640
CONV_PAD = 16
CONV_ROWS = 64
NORM_ROWS = 64
TQ = 256
VMEM_LIMIT = 56 * 1024 * 1024

IN_AH, IN_AB, IN_AC = 0, A_W, 2 * A_W
IN_Q = 3 * A_W
IN_KV = IN_Q + Q_RANK
IN_GLU = IN_KV + KV_RANK + ROPE_DIM
TAIL_G1, TAIL_G2, TAIL_GATE = 0, C_W, 2 * C_W
TAIL_W = 2 * C_W + 3 * D_MODEL

F32 = jnp.float32
BF16 = jnp.bfloat16


def _params(n_axes):
    return pltpu.CompilerParams(dimension_semantics=("arbitrary",) * n_axes,
                                vmem_limit_bytes=VMEM_LIMIT)


def _cond_of_row(row):
    return jnp.maximum(row // DEC_SEQ - (CTX_TOK // DEC_SEQ - 1), 0)


def _sigmoid(x):
    return 0.5 * jnp.tanh(0.5 * x) + 0.5


def _silu(x):
    return x * _sigmoid(x)


def _bdot(a, w):
    return jnp.dot(a, w.astype(BF16), preferred_element_type=F32)


def _bdot_t(a, wt):
    return lax.dot_general(a, wt.astype(BF16), (((1,), (1,)), ((), ())), preferred_element_type=F32)


def _conv3_rows(u, w, is_ctx):
    n, width = u.shape
    inside = jnp.where(is_ctx, 0.0, 1.0)
    sub = lax.broadcasted_iota(jnp.int32, (SUBLANES, width), 0)
    zeros = jnp.zeros((SUBLANES, width), F32)
    pieces = []
    for r0 in range(0, n, SEQ):
        up = u[r0:r0 + SEQ]
        prev = pltpu.roll(up, 1, 0)
        nxt = pltpu.roll(up, SEQ - 1, 0)
        before = pltpu.roll(u[r0 - SUBLANES:r0], 1, 0) * inside if r0 % DEC_SEQ else zeros
        after = (pltpu.roll(u[r0 + SEQ:r0 + SEQ + SUBLANES], SUBLANES - 1, 0) * inside
                 if (r0 + SEQ) % DEC_SEQ else zeros)
        prev = jnp.concatenate([jnp.where(sub == 0, before, prev[:SUBLANES]), prev[SUBLANES:]], axis=0)
        nxt = jnp.concatenate([nxt[:SEQ - SUBLANES],
                               jnp.where(sub == SUBLANES - 1, after, nxt[SEQ - SUBLANES:])], axis=0)
        pieces.append(w[0:1, :] * prev + w[1:2, :] * up + w[2:3, :] * nxt)
    return jnp.concatenate(pieces, axis=0)


def _mod_kernel(c_ref, w_ref, b_ref, o_ref):
    s = _silu(c_ref[...]).astype(BF16)
    o_ref[...] = _bdot(s, w_ref[...]) + b_ref[...]


def _modulation(cond, w_mod, b_mod):
    tn = 1024
    n = 6 * D_MODEL
    return pl.pallas_call(
        _mod_kernel,
        grid=(DEPTH, n // tn),
        in_specs=[pl.BlockSpec((COND_ROWS, D_MODEL), lambda l, j: (0, 0)),
                  pl.BlockSpec((None, D_MODEL, tn), lambda l, j: (l, 0, j)),
                  pl.BlockSpec((None, 1, tn), lambda l, j: (l, 0, j))],
        out_specs=pl.BlockSpec((None, COND_ROWS, tn), lambda l, j: (l, 0, j)),
        out_shape=jax.ShapeDtypeStruct((DEPTH, COND_ROWS, n), F32),
        compiler_params=_params(2),
        name="modulation",
    )(cond, w_mod, b_mod.reshape(DEPTH, 1, n))


def _rms(x):
    return x * lax.rsqrt(jnp.mean(x * x, axis=-1, keepdims=True) + EPS)


def _stream_specs(xs, block, col):
    if len(xs) == 1:
        return [pl.BlockSpec(block, lambda i, *j: (i, col(*j)))], None
    n_first = xs[0].shape[0] // block[0]
    return [pl.BlockSpec(block, lambda i, *j: (jnp.minimum(i, n_first - 1),
                                               jnp.where(i < n_first, col(*j), 0))),
            pl.BlockSpec(block, lambda i, *j: (jnp.maximum(i - n_first, 0),
                                               jnp.where(i < n_first, 0, col(*j))))], n_first


def _stream_block(x_refs, n_first):
    x = x_refs[0][...]
    if len(x_refs) == 2:
        x = jnp.where(pl.program_id(0) < n_first, x, x_refs[1][...])
    return x


def _norm_mod_kernel(*refs, n_first):
    *x_refs, g_ref, sc_ref, sh_ref, o_ref = refs
    y = _rms(_stream_block(x_refs, n_first)) * g_ref[...]
    o_ref[...] = (y * (1.0 + sc_ref[...]) + sh_ref[...]).astype(o_ref.dtype)


def _norm_mod(xs, g, sc, sh):
    tr = TM
    x_specs, n_first = _stream_specs(xs, (tr, D_MODEL), lambda: 0)
    cond = lambda i: (_cond_of_row(i * tr), 0, 0)
    return pl.pallas_call(
        functools.partial(_norm_mod_kernel, n_first=n_first),
        grid=(TOK // tr,),
        in_specs=x_specs + [pl.BlockSpec((1, D_MODEL), lambda i: (0, 0)),
                            pl.BlockSpec((None, 1, D_MODEL), cond),
                            pl.BlockSpec((None, 1, D_MODEL), cond)],
        out_specs=pl.BlockSpec((tr, D_MODEL), lambda i: (i, 0)),
        out_shape=jax.ShapeDtypeStruct((TOK, D_MODEL), BF16),
        compiler_params=_params(1),
        name="norm_mod",
    )(*xs, g, sc, sh)


def _final_norm_kernel(x_ref, g_ref, o_ref):
    o_ref[...] = _rms(x_ref[...]) * g_ref[...]


def _final_norm(x, g, row0, rows):
    tr = TM
    return pl.pallas_call(
        _final_norm_kernel,
        grid=(rows // tr,),
        in_specs=[pl.BlockSpec((tr, D_MODEL), lambda i: (row0 // tr + i, 0)),
                  pl.BlockSpec((1, D_MODEL), lambda i: (0, 0))],
        out_specs=pl.BlockSpec((tr, D_MODEL), lambda i: (i, 0)),
        out_shape=jax.ShapeDtypeStruct((rows, D_MODEL), F32),
        compiler_params=_params(1),
        name="final_norm",
    )(x, g)


def _mm_residual_kernel(*refs, n_first):
    a_ref, w_ref, *x_refs, gt_lo_ref, gt_hi_ref, o_ref = refs
    x = _stream_block(x_refs, n_first)
    acc = _bdot(a_ref[...], w_ref[...])
    half = acc.shape[0] // 2
    o_ref[:half, :] = x[:half] + gt_lo_ref[...] * acc[:half]
    o_ref[half:, :] = x[half:] + gt_hi_ref[...] * acc[half:]


def _matmul_residual(a, w, l, xs, gt, tn, tm, name):
    m, k = a.shape
    n = w.shape[2]
    x_specs, n_first = _stream_specs(xs, (tm, tn), lambda j: j)
    return pl.pallas_call(
        functools.partial(_mm_residual_kernel, n_first=n_first),
        grid=(m // tm, n // tn),
        in_specs=[pl.BlockSpec((tm, k), lambda i, j: (i, 0)),
                  pl.BlockSpec((None, k, tn), lambda i, j: (l, 0, j))] + x_specs + [
                  pl.BlockSpec((None, 1, tn), lambda i, j: (_cond_of_row(i * tm), 0, j)),
                  pl.BlockSpec((None, 1, tn), lambda i, j: (_cond_of_row(i * tm + tm // 2), 0, j))],
        out_specs=pl.BlockSpec((tm, tn), lambda i, j: (i, j)),
        out_shape=jax.ShapeDtypeStruct((m, n), F32),
        compiler_params=_params(2),
        name=name,
    )(a, w, *xs, gt, gt)


def _tail_kernel(h_ref, wt_ref, o_ref):
    o_ref[...] = _bdot_t(h_ref[...], wt_ref[...]).astype(o_ref.dtype)


def _tail_proj(h, w_in_t, l):
    tn = 512
    in_w = w_in_t.shape[1]
    return pl.pallas_call(
        _tail_kernel,
        grid=(TOK // TMW, TAIL_W // tn),
        in_specs=[pl.BlockSpec((TMW, D_MODEL), lambda i, j: (i, 0)),
                  pl.BlockSpec((pl.Element(tn), pl.Element(D_MODEL)),
                               lambda i, j: (pl.multiple_of(l * in_w + IN_GLU + j * tn, SUBLANES), 0))],
        out_specs=pl.BlockSpec((TMW, tn), lambda i, j: (i, j)),
        out_shape=jax.ShapeDtypeStruct((TOK, TAIL_W), BF16),
        compiler_params=_params(2),
        name="tail_proj",
    )(h, w_in_t.reshape(DEPTH * in_w, D_MODEL))


def _mixer_a_kernel(h_ref, wh_ref, wb_ref, wc_ref, cw_ref, o_ref):
    is_ctx = pl.program_id(0) < CTX_TOK // TMW
    h = h_ref[...]
    p = _bdot_t(h, wc_ref[...]) * _bdot_t(h, wh_ref[...])
    o_ref[...] = (_bdot_t(h, wb_ref[...]) * _conv3_rows(p, cw_ref[...], is_ctx)).astype(o_ref.dtype)


def _mixer_a(h, w_in, conv_a, l):
    tn = 256
    col = lambda off: (lambda i, j: (l, off // tn + j, 0))
    wspec = lambda off: pl.BlockSpec((None, tn, D_MODEL), col(off))
    return pl.pallas_call(
        _mixer_a_kernel,
        grid=(TOK // TMW, A_W // tn),
        in_specs=[pl.BlockSpec((TMW, D_MODEL), lambda i, j: (i, 0)),
                  wspec(IN_AH), wspec(IN_AB), wspec(IN_AC),
                  pl.BlockSpec((None, A_KERNEL, tn), lambda i, j: (l, 0, j))],
        out_specs=pl.BlockSpec((TMW, tn), lambda i, j: (i, j)),
        out_shape=jax.ShapeDtypeStruct((TOK, A_W), BF16),
        compiler_params=_params(2),
        name="mixer_a",
    )(h, w_in, w_in, w_in, conv_a)


def _q_kernel(h_ref, wa_ref, g_ref, w_ref, o_ref, qn_ref):
    @pl.when(pl.program_id(1) == 0)
    def _():
        qn_ref[...] = (_rms(_bdot_t(h_ref[...], wa_ref[...])) * g_ref[...]).astype(BF16)

    o_ref[...] = _bdot(qn_ref[...], w_ref[...]).astype(o_ref.dtype)


def _q_proj(h, w_in, g_q, w_q, l):
    tn = 1024
    n = w_q.shape[2]
    return pl.pallas_call(
        _q_kernel,
        grid=(TOK // TMW, n // tn),
        in_specs=[pl.BlockSpec((TMW, D_MODEL), lambda i, j: (i, 0)),
                  pl.BlockSpec((None, Q_RANK, D_MODEL), lambda i, j: (l, IN_Q // Q_RANK, 0),
                               pipeline_mode=pl.Buffered(1)),
                  pl.BlockSpec((1, Q_RANK), lambda i, j: (0, 0)),
                  pl.BlockSpec((None, Q_RANK, tn), lambda i, j: (l, 0, j))],
        out_specs=pl.BlockSpec((TMW, tn), lambda i, j: (i, j)),
        out_shape=jax.ShapeDtypeStruct((TOK, n), BF16),
        scratch_shapes=[pltpu.VMEM((TMW, Q_RANK), BF16)],
        compiler_params=_params(2),
        name="q_proj",
    )(h, w_in, g_q, w_q)


def _kv_kernel(h_ref, wa_ref, g_ref, w_ref, ckv_ref, kpe_ref, kv_ref, cn_ref):
    @pl.when(pl.program_id(1) == 0)
    def _():
        kv_a = _bdot_t(h_ref[...], wa_ref[...])
        y = _rms(kv_a[:, :KV_RANK]) * g_ref[...]
        kpe_ref[...] = kv_a[:, KV_RANK:]
        cn_ref[...] = y.astype(BF16)

        @pl.when(pl.program_id(0) < CTX_TOK // TMW)
        def _():
            ckv_ref[...] = y

    kv_ref[...] = _bdot(cn_ref[...], w_ref[...]).astype(kv_ref.dtype)


def _kv_proj(h, w_in, g_kv, w_ukv, l):
    tn = 512
    n = w_ukv.shape[2]
    last_ctx = CTX_TOK // TMW - 1
    return pl.pallas_call(
        _kv_kernel,
        grid=(TOK // TMW, n // tn),
        in_specs=[pl.BlockSpec((TMW, D_MODEL), lambda i, j: (i, 0)),
                  pl.BlockSpec((None, KV_A_W, D_MODEL), lambda i, j: (l, IN_KV // KV_A_W, 0),
                               pipeline_mode=pl.Buffered(1)),
                  pl.BlockSpec((1, KV_RANK), lambda i, j: (0, 0)),
                  pl.BlockSpec((None, KV_RANK, tn), lambda i, j: (l, 0, j))],
        out_specs=[pl.BlockSpec((TMW, KV_RANK), lambda i, j: (jnp.minimum(i, last_ctx), 0)),
                   pl.BlockSpec((TMW, LANES), lambda i, j: (i, 0)),
                   pl.BlockSpec((TMW, tn), lambda i, j: (i, j))],
        out_shape=[jax.ShapeDtypeStruct((CTX_TOK, KV_RANK), F32),
                   jax.ShapeDtypeStruct((TOK, LANES), F32),
                   jax.ShapeDtypeStruct((TOK, n), BF16)],
        scratch_shapes=[pltpu.VMEM((TMW, KV_RANK), BF16)],
        compiler_params=_params(2),
        name="kv_proj",
    )(h, w_in, g_kv, w_ukv)


def _cache_kv_kernel(a_ref, w_ref, o_ref):
    o_ref[...] = _bdot(a_ref[...].astype(BF16), w_ref[...]).astype(o_ref.dtype)


def _cache_kv_proj(cache_ckv, w_ukv, l):
    tn = 1024
    n = w_ukv.shape[2]
    return pl.pallas_call(
        _cache_kv_kernel,
        grid=(DEC_BATCH, n // tn),
        in_specs=[pl.BlockSpec((None, None, PAST_LEN, KV_RANK), lambda b, j: (b, l, 0, 0)),
                  pl.BlockSpec((None, KV_RANK, tn), lambda b, j: (l, 0, j))],
        out_specs=pl.BlockSpec((PAST_LEN, tn), lambda b, j: (b, j)),
        out_shape=jax.ShapeDtypeStruct((DEC_BATCH * PAST_LEN, n), BF16),
        compiler_params=_params(2),
        name="cache_kv_proj",
    )(cache_ckv, w_ukv)


def _merge_kernel(a1_ref, w1_ref, a2_ref, w2_ref, a3_ref, w3_ref, g1_ref, g2_ref, g3_ref, o_ref):
    def branch(a_ref, w_ref, g_ref):
        return (1.0 + jnp.tanh(0.5 * g_ref[...].astype(F32))) * _bdot(a_ref[...], w_ref[...])

    o_ref[...] = (0.5 * (branch(a1_ref, w1_ref, g1_ref) + branch(a2_ref, w2_ref, g2_ref)
                         + branch(a3_ref, w3_ref, g3_ref))).astype(o_ref.dtype)


def _merge(ya, w_a_out, attn, w_mla_out, yc, w_c_out, tail, l):
    tn = 256
    gate = lambda k: (lambda i, j: (i, (TAIL_GATE + k * D_MODEL) // tn + j))
    act = lambda width: pl.BlockSpec((TM, width), lambda i, j: (i, 0))
    wgt = lambda width: pl.BlockSpec((None, width, tn), lambda i, j: (l, 0, j))
    return pl.pallas_call(
        _merge_kernel,
        grid=(N_ROW_BLOCKS, D_MODEL // tn),
        in_specs=[act(A_W), wgt(A_W), act(N_HEADS * V_DIM), wgt(N_HEADS * V_DIM), act(C_W), wgt(C_W),
                  pl.BlockSpec((TM, tn), gate(0)), pl.BlockSpec((TM, tn), gate(1)),
                  pl.BlockSpec((TM, tn), gate(2))],
        out_specs=pl.BlockSpec((TM, tn), lambda i, j: (i, j)),
        out_shape=jax.ShapeDtypeStruct((TOK, D_MODEL), BF16),
        compiler_params=_params(2),
        name="merge",
    )(ya, w_a_out, attn, w_mla_out, yc, w_c_out, tail, tail, tail)


FFN_TN = 512
FFN_COL_TILES = D_FF // FFN_TN


def _ffn_up_kernel(h_ref, wa_ref, wg_ref, ca_ref, cg_ref, o_ref):
    is_ctx = pl.program_id(0) < CTX_BLOCKS
    h = h_ref[...]
    ua = _bdot(h, wa_ref[...])
    ug = _bdot(h, wg_ref[...])
    a = _conv3_rows(ua, ca_ref[...], is_ctx)
    g = _conv3_rows(ug, cg_ref[...], is_ctx)
    o_ref[...] = (_silu(a) * g).astype(o_ref.dtype)


def _ffn_up(h, w_up, conv_f, l):
    tn = FFN_TN
    half = FFN_COL_TILES
    return pl.pallas_call(
        _ffn_up_kernel,
        grid=(N_ROW_BLOCKS, half),
        in_specs=[pl.BlockSpec((TM, D_MODEL), lambda i, j: (i, 0)),
                  pl.BlockSpec((None, D_MODEL, tn), lambda i, j: (l, 0, j)),
                  pl.BlockSpec((None, D_MODEL, tn), lambda i, j: (l, 0, half + j)),
                  pl.BlockSpec((None, F_KERNEL, tn), lambda i, j: (l, 0, j)),
                  pl.BlockSpec((None, F_KERNEL, tn), lambda i, j: (l, 0, half + j))],
        out_specs=pl.BlockSpec((TM, tn), lambda i, j: (i, j)),
        out_shape=jax.ShapeDtypeStruct((TOK, D_FF), BF16),
        compiler_params=_params(2),
        name="ffn_up",
    )(h, w_up, w_up, conv_f, conv_f)


def _conv_c_kernel(g1_ref, g2_ref, w_ref, lng_ref, lnb_ref, o_ref, u_ref, sh_ref, cv_ref):
    is_dec = pl.program_id(0) >= CTX_BLOCKS
    halo_w = SEQ + 2 * CONV_PAD
    n_pieces = TM // SEQ

    def gated(r0, n):
        return g1_ref[r0:r0 + n, :].astype(F32) * _sigmoid(g2_ref[r0:r0 + n, :].astype(F32))

    zeros = jnp.zeros((CONV_PAD, C_W), F32)
    u_ref[halo_w:halo_w + SUBLANES, :] = zeros[:SUBLANES]

    for p in range(n_pieces):
        r0 = p * SEQ
        top = jnp.where(is_dec, gated(r0 - CONV_PAD, CONV_PAD), 0.0) if p > 0 else zeros
        bot = jnp.where(is_dec, gated(r0 + SEQ, CONV_PAD), 0.0) if p < n_pieces - 1 else zeros
        u_ref[0:CONV_PAD, :] = top
        u_ref[CONV_PAD:CONV_PAD + SEQ, :] = gated(r0, SEQ)
        u_ref[CONV_PAD + SEQ:halo_w, :] = bot

        def shift_rows(c, carry):
            base = pl.multiple_of(c * SUBLANES, SUBLANES)
            win = u_ref[pl.ds(base, 2 * SUBLANES), :]
            for b in range(1, SUBLANES):
                sh_ref[b - 1, pl.ds(base, SUBLANES), :] = pltpu.roll(win, 2 * SUBLANES - b, 0)[:SUBLANES]
            return carry

        lax.fori_loop(0, halo_w // SUBLANES, shift_rows, 0)

        for s in range(C_W // LANES):
            cols = slice(s * LANES, (s + 1) * LANES)
            taps = [jnp.broadcast_to(w_ref[k:k + 1, cols], (SUBLANES, LANES)) for k in range(C_KERNEL)]

            def conv_rows(c, carry, cols=cols, taps=taps):
                base = pl.multiple_of(c * CONV_ROWS, CONV_ROWS)
                for r in range(0, CONV_ROWS, SUBLANES):
                    acc = None
                    for k in range(C_KERNEL):
                        off = CONV_PAD - C_KERNEL // 2 + k
                        start = pl.multiple_of(base + r + (off // SUBLANES) * SUBLANES, SUBLANES)
                        b = off % SUBLANES
                        shifted = sh_ref.at[b - 1] if b else u_ref
                        term = taps[k] * shifted[pl.ds(start, SUBLANES), cols]
                        acc = term if acc is None else acc + term
                    cv_ref[pl.ds(base + r, SUBLANES), cols] = acc
                return carry

            lax.fori_loop(0, SEQ // CONV_ROWS, conv_rows, 0)

        for base in range(0, SEQ, NORM_ROWS):
            v = cv_ref[base:base + NORM_ROWS, :]
            d = v - jnp.mean(v, axis=-1, keepdims=True)
            var = jnp.mean(d * d, axis=-1, keepdims=True)
            half_y = d * lax.rsqrt(var + EPS) * (0.5 * lng_ref[...]) + 0.5 * lnb_ref[...]
            o_ref[r0 + base:r0 + base + NORM_ROWS, :] = (half_y * (1.0 + jnp.tanh(half_y))).astype(o_ref.dtype)


def _conv_c(tail, conv_c, ln_g, ln_b, l):
    halo_w = SEQ + 2 * CONV_PAD
    return pl.pallas_call(
        _conv_c_kernel,
        grid=(N_ROW_BLOCKS,),
        in_specs=[pl.BlockSpec((TM, C_W), lambda i: (i, TAIL_G1 // C_W)),
                  pl.BlockSpec((TM, C_W), lambda i: (i, TAIL_G2 // C_W)),
                  pl.BlockSpec((None, C_KERNEL, C_W), lambda i: (l, 0, 0)),
                  pl.BlockSpec((1, C_W), lambda i: (0, 0)),
                  pl.BlockSpec((1, C_W), lambda i: (0, 0))],
        out_specs=pl.BlockSpec((TM, C_W), lambda i: (i, 0)),
        out_shape=jax.ShapeDtypeStruct((TOK, C_W), BF16),
        scratch_shapes=[pltpu.VMEM((halo_w + SUBLANES, C_W), F32),
                        pltpu.VMEM((SUBLANES - 1, halo_w, C_W), F32),
                        pltpu.VMEM((SEQ, C_W), F32)],
        compiler_params=_params(1),
        name="conv_c",
    )(tail, tail, conv_c, ln_g, ln_b)


def _rope_pairs(x, cos, sin):
    half = ROPE_DIM // 4
    lane = lax.broadcasted_iota(jnp.int32, x.shape, 1)
    swapped = jnp.where((lane & (2 * half - 1)) < half,
                        pltpu.roll(x, LANES - half, 1), pltpu.roll(x, half, 1))
    return x * cos + swapped * sin


def _pe_keys(k):
    lane = lax.broadcasted_iota(jnp.int32, k.shape, 1)
    lo = jnp.where(lane < ROPE_DIM, k, 0.0)
    return lo.astype(BF16), pltpu.roll(lo, ROPE_DIM, 1).astype(BF16)


def _attend(q_ref, rows, segments, o_ref, rope_q=None):
    c = QK_DIM ** -0.5 * math.log2(math.e)
    pe_col0 = N_HEADS * NOPE_DIM
    ones = [jnp.ones((pe[0].shape[0], V_DIM), BF16) for _, _, pe in segments]
    for hp in range(HEAD_PAIRS):
        qp = q_ref[rows, pe_col0 + hp * LANES:pe_col0 + (hp + 1) * LANES]
        if rope_q is not None:
            qp = _rope_pairs(qp.astype(F32), *rope_q).astype(BF16)
        for h in range(2):
            head = 2 * hp + h
            q = jnp.concatenate([q_ref[rows, head * NOPE_DIM:(head + 1) * NOPE_DIM], qp], axis=1)
            kbase = head * (NOPE_DIM + V_DIM)
            scores = []
            for kv_ref, krows, pe in segments:
                k = jnp.concatenate([kv_ref[krows, kbase:kbase + NOPE_DIM], pe[h]], axis=1)
                scores.append(lax.dot_general(q, k, (((1,), (1,)), ((), ())),
                                              preferred_element_type=F32))
            m = scores[0].max(axis=-1, keepdims=True)
            for s in scores[1:]:
                m = jnp.maximum(m, s.max(axis=-1, keepdims=True))
            acc = 0.0
            for s, (kv_ref, krows, _), one in zip(scores, segments, ones):
                p = jnp.exp2((s - m) * c)
                v = jnp.concatenate([kv_ref[krows, kbase + NOPE_DIM:kbase + NOPE_DIM + V_DIM], one], axis=1)
                acc = acc + jnp.dot(p.astype(BF16), v, preferred_element_type=F32)
            o_ref[rows, head * V_DIM:(head + 1) * V_DIM] = (acc[:, :V_DIM] / acc[:, V_DIM:]).astype(o_ref.dtype)


def _attn_kernel(q_ref, kvx_ref, kpex_ref, kvd_ref, kped_ref, kvc_ref, kpec_ref,
                 cq_ref, sq_ref, ck_ref, sk_ref, o_ref):
    step = pl.program_id(0)
    everything = slice(None)

    @pl.when(step < CTX_TOK // TQ)
    def _():
        for r0 in range(0, TQ, SEQ):
            rows = slice(r0, r0 + SEQ)
            _attend(q_ref, rows, [(kvx_ref, rows, _pe_keys(kpex_ref[rows, :]))], o_ref)

    @pl.when(step >= CTX_TOK // TQ)
    def _():
        new_pe = _rope_pairs(kped_ref[...], ck_ref[...], sk_ref[...])
        _attend(q_ref, everything,
                [(kvc_ref, everything, _pe_keys(kpec_ref[...])), (kvd_ref, everything, _pe_keys(new_pe))],
                o_ref, rope_q=(cq_ref[...], sq_ref[...]))


def _attention(q, kv, kpe, kv_cache, kpe_cache, cos, sin):
    per_seq = DEC_SEQ // TQ
    ctx_steps = CTX_TOK // TQ
    n_steps = ctx_steps + DEC_BATCH * per_seq
    ctx = lambda s: jnp.minimum(s, ctx_steps - 1)
    dec = lambda s: jnp.maximum(s - ctx_steps, 0)
    kv_w = N_HEADS * (NOPE_DIM + V_DIM)
    in_specs = [
        pl.BlockSpec((TQ, q.shape[1]), lambda s: (s, 0)),
        pl.BlockSpec((TQ, kv_w), lambda s: (ctx(s), 0)),
        pl.BlockSpec((TQ, LANES), lambda s: (ctx(s), 0)),
        pl.BlockSpec((DEC_SEQ, kv_w), lambda s: (CTX_TOK // DEC_SEQ + dec(s) // per_seq, 0)),
        pl.BlockSpec((DEC_SEQ, LANES), lambda s: (CTX_TOK // DEC_SEQ + dec(s) // per_seq, 0)),
        pl.BlockSpec((PAST_LEN, kv_w), lambda s: (dec(s) // per_seq, 0)),
        pl.BlockSpec((None, PAST_LEN, LANES), lambda s: (dec(s) // per_seq, 0, 0)),
        pl.BlockSpec((TQ, LANES), lambda s: (dec(s) % per_seq, 0)),
        pl.BlockSpec((TQ, LANES), lambda s: (dec(s) % per_seq, 0)),
        pl.BlockSpec((DEC_SEQ, LANES), lambda s: (0, 0)),
        pl.BlockSpec((DEC_SEQ, LANES), lambda s: (0, 0)),
    ]
    return pl.pallas_call(
        _attn_kernel,
        grid=(n_steps,),
        in_specs=in_specs,
        out_specs=pl.BlockSpec((TQ, N_HEADS * V_DIM), lambda s: (s, 0)),
        out_shape=jax.ShapeDtypeStruct((TOK, N_HEADS * V_DIM), BF16),
        compiler_params=_params(1),
        name="attention",
    )(q, kv, kpe, kv, kpe, kv_cache, kpe_cache, cos, sin, cos, sin)


def _rope_tables():
    rows = DEC_SEQ // GRID_W
    r = jnp.repeat(jnp.arange(rows), GRID_W).astype(F32)
    col = jnp.tile(jnp.arange(GRID_W), rows).astype(F32)
    half_axis = ROPE_DIM // 2
    inv = ROPE_THETA ** (-jnp.arange(0, half_axis, 2, dtype=F32) / half_axis)
    ar = r[:, None] * inv
    ac = col[:, None] * inv
    cos = jnp.concatenate([jnp.cos(ar), jnp.cos(ar), jnp.cos(ac), jnp.cos(ac)], axis=-1)
    sin = jnp.concatenate([-jnp.sin(ar), jnp.sin(ar), -jnp.sin(ac), jnp.sin(ac)], axis=-1)
    reps = LANES // ROPE_DIM
    return jnp.tile(cos, (1, reps)), jnp.tile(sin, (1, reps))


def kernel(x_prompt, x_sample, cache_ckv, cache_kpe, c, c_ctx, w_mod, b_mod, g_norm1, g_norm2, w_in, conv_a, w_a_out, g_q, w_uq, g_kv, w_ukv, w_mla_out, conv_c, ln_c_g, ln_c_b, w_c_out, w_o, w_up, conv_f, w_down, g_final):
    xs = (x_prompt.reshape(CTX_TOK, D_MODEL), x_sample.reshape(DEC_TOK, D_MODEL))
    cond = jnp.concatenate([c_ctx[None, :], c, jnp.zeros((COND_ROWS - N_COND, D_MODEL), F32)], axis=0)
    mod = _modulation(cond, w_mod, b_mod)
    cos, sin = _rope_tables()

    w_in_t = jnp.swapaxes(w_in, 1, 2)
    wq = w_uq.reshape(DEPTH, Q_RANK, N_HEADS, QK_DIM)
    w_q = jnp.concatenate([wq[..., :NOPE_DIM].reshape(DEPTH, Q_RANK, N_HEADS * NOPE_DIM),
                           wq[..., NOPE_DIM:].reshape(DEPTH, Q_RANK, N_HEADS * ROPE_DIM)],
                          axis=2).astype(BF16)
    kpe_cache = jnp.pad(cache_kpe, ((0, 0), (0, 0), (0, 0), (0, LANES - ROPE_DIM)))
    ckv_layers, kpe_layers = [], []
    for l in range(DEPTH):
        sh1, sc1, gt1, sh2, sc2, gt2 = [
            mod[l, :N_COND, k * D_MODEL:(k + 1) * D_MODEL].reshape(N_COND, 1, D_MODEL) for k in range(6)]

        h = _norm_mod(xs, g_norm1[l][None, :], sc1, sh1)
        ya = _mixer_a(h, w_in_t, conv_a, l)
        q = _q_proj(h, w_in_t, g_q[l][None, :], w_q, l)
        ckv, kpe, kv = _kv_proj(h, w_in_t, g_kv[l][None, :], w_ukv, l)
        tail = _tail_proj(h, w_in_t, l)
        kv_cache = _cache_kv_proj(cache_ckv, w_ukv, l)
        attn = _attention(q, kv, kpe, kv_cache, kpe_cache[:, l], cos, sin)
        yc = _conv_c(tail, conv_c, ln_c_g[l][None, :], ln_c_b[l][None, :], l)
        merged = _merge(ya, w_a_out, attn, w_mla_out, yc, w_c_out, tail, l)
        xs = (_matmul_residual(merged, w_o, l, xs, gt1, tn=512, tm=TMW, name="out_proj"),)

        h = _norm_mod(xs, g_norm2[l][None, :], sc2, sh2)
        act = _ffn_up(h, w_up, conv_f, l)
        xs = (_matmul_residual(act, w_down, l, xs, gt2, tn=256, tm=TM, name="ffn_down"),)

        ckv_layers.append(ckv.reshape(BATCH, SEQ, KV_RANK))
        kpe_layers.append(kpe[:CTX_TOK, :ROPE_DIM].reshape(BATCH, SEQ, ROPE_DIM))

    g = g_final[None, :]
    y_prompt = _final_norm(xs[0], g, 0, CTX_TOK).reshape(BATCH, SEQ, D_MODEL)
    y_sample = _final_norm(xs[0], g, CTX_TOK, DEC_TOK).reshape(DEC_BATCH, DEC_SEQ, D_MODEL)
    return (y_prompt, y_sample, jnp.stack(ckv_layers, axis=1), jnp.stack(kpe_layers, axis=1))
```

```python
import functools
import math

import jax
import jax.numpy as jnp
from jax import lax
from jax.experimental import pallas as pl
from jax.experimental.pallas import tpu as pltpu

D_MODEL = 2048
BATCH = 16
SEQ = 256
DEPTH = 2
DEC_BATCH = 2
DEC_SEQ = 1024
PAST_LEN = 256
GRID_W = 64
A_W = 1024
A_KERNEL = 3
N_HEADS = 16
Q_RANK = 768
KV_RANK = 512
NOPE_DIM = 128
ROPE_DIM = 64
V_DIM = 128
QK_DIM = NOPE_DIM + ROPE_DIM
C_W = 1024
C_KERNEL = 31
D_FF = 5632
F_KERNEL = 3
ROPE_THETA = 10000.0
EPS = 1e-6

CTX_TOK = BATCH * SEQ
DEC_TOK = DEC_BATCH * DEC_SEQ
TOK = CTX_TOK + DEC_TOK
TM = 1024
N_ROW_BLOCKS = TOK // TM
CTX_BLOCKS = CTX_TOK // TM
TMW = 2048
N_COND = 3
COND_ROWS = 8
LANES = 128
SUBLANES = 8
HEAD_PAIRS = N_HEADS // 2
KV_A_W = 640
CONV_PAD = 16
CONV_ROWS = 64
NORM_ROWS = 64
TQ = 256
VMEM_LIMIT = 56 * 1024 * 1024

IN_AH, IN_AB, IN_AC = 0, A_W, 2 * A_W
IN_Q = 3 * A_W
IN_KV = IN_Q + Q_RANK
IN_GLU = IN_KV + KV_RANK + ROPE_DIM
TAIL_G1, TAIL_G2, TAIL_GATE = 0, C_W, 2 * C_W
TAIL_W = 2 * C_W + 3 * D_MODEL

F32 = jnp.float32
BF16 = jnp.bfloat16


def _params(n_axes):
    return pltpu.CompilerParams(dimension_semantics=("arbitrary",) * n_axes,
                                vmem_limit_bytes=VMEM_LIMIT)


def _cond_of_row(row):
    return jnp.maximum(row // DEC_SEQ - (CTX_TOK // DEC_SEQ - 1), 0)


def _sigmoid(x):
    return 0.5 * jnp.tanh(0.5 * x) + 0.5


def _silu(x):
    return x * _sigmoid(x)


def _bdot(a, w):
    return jnp.dot(a, w.astype(BF16), preferred_element_type=F32)


def _bdot_t(a, wt):
    return lax.dot_general(a, wt.astype(BF16), (((1,), (1,)), ((), ())), preferred_element_type=F32)


def _conv3_rows(u, w, is_ctx):
    n, width = u.shape
    inside = jnp.where(is_ctx, 0.0, 1.0)
    sub = lax.broadcasted_iota(jnp.int32, (SUBLANES, width), 0)
    zeros = jnp.zeros((SUBLANES, width), F32)
    pieces = []
    for r0 in range(0, n, SEQ):
        up = u[r0:r0 + SEQ]
        prev = pltpu.roll(up, 1, 0)
        nxt = pltpu.roll(up, SEQ - 1, 0)
        before = pltpu.roll(u[r0 - SUBLANES:r0], 1, 0) * inside if r0 % DEC_SEQ else zeros
        after = (pltpu.roll(u[r0 + SEQ:r0 + SEQ + SUBLANES], SUBLANES - 1, 0) * inside
                 if (r0 + SEQ) % DEC_SEQ else zeros)
        prev = jnp.concatenate([jnp.where(sub == 0, before, prev[:SUBLANES]), prev[SUBLANES:]], axis=0)
        nxt = jnp.concatenate([nxt[:SEQ - SUBLANES],
                               jnp.where(sub == SUBLANES - 1, after, nxt[SEQ - SUBLANES:])], axis=0)
        pieces.append(w[0:1, :] * prev + w[1:2, :] * up + w[2:3, :] * nxt)
    return jnp.concatenate(pieces, axis=0)


def _mod_kernel(c_ref, w_ref, b_ref, o_ref):
    s = _silu(c_ref[...]).astype(BF16)
    o_ref[...] = _bdot(s, w_ref[...]) + b_ref[...]


def _modulation(cond, w_mod, b_mod):
    tn = 1024
    n = 6 * D_MODEL
    return pl.pallas_call(
        _mod_kernel,
        grid=(DEPTH, n // tn),
        in_specs=[pl.BlockSpec((COND_ROWS, D_MODEL), lambda l, j: (0, 0)),
                  pl.BlockSpec((None, D_MODEL, tn), lambda l, j: (l, 0, j)),
                  pl.BlockSpec((None, 1, tn), lambda l, j: (l, 0, j))],
        out_specs=pl.BlockSpec((None, COND_ROWS, tn), lambda l, j: (l, 0, j)),
        out_shape=jax.ShapeDtypeStruct((DEPTH, COND_ROWS, n), F32),
        compiler_params=_params(2),
        name="modulation",
    )(cond, w_mod, b_mod.reshape(DEPTH, 1, n))


def _rms(x):
    return x * lax.rsqrt(jnp.mean(x * x, axis=-1, keepdims=True) + EPS)


def _stream_specs(xs, block, col):
    if len(xs) == 1:
        return [pl.BlockSpec(block, lambda i, *j: (i, col(*j)))], None
    n_first = xs[0].shape[0] // block[0]
    return [pl.BlockSpec(block, lambda i, *j: (jnp.minimum(i, n_first - 1),
                                               jnp.where(i < n_first, col(*j), 0))),
            pl.BlockSpec(block, lambda i, *j: (jnp.maximum(i - n_first, 0),
                                               jnp.where(i < n_first, 0, col(*j))))], n_first


def _stream_block(x_refs, n_first):
    x = x_refs[0][...]
    if len(x_refs) == 2:
        x = jnp.where(pl.program_id(0) < n_first, x, x_refs[1][...])
    return x


def _norm_mod_kernel(*refs, n_first):
    *x_refs, g_ref, sc_ref, sh_ref, o_ref = refs
    y = _rms(_stream_block(x_refs, n_first)) * g_ref[...]
    o_ref[...] = (y * (1.0 + sc_ref[...]) + sh_ref[...]).astype(o_ref.dtype)


def _norm_mod(xs, g, sc, sh):
    tr = TM
    x_specs, n_first = _stream_specs(xs, (tr, D_MODEL), lambda: 0)
    cond = lambda i: (_cond_of_row(i * tr), 0, 0)
    return pl.pallas_call(
        functools.partial(_norm_mod_kernel, n_first=n_first),
        grid=(TOK // tr,),
        in_specs=x_specs + [pl.BlockSpec((1, D_MODEL), lambda i: (0, 0)),
                            pl.BlockSpec((None, 1, D_MODEL), cond),
                            pl.BlockSpec((None, 1, D_MODEL), cond)],
        out_specs=pl.BlockSpec((tr, D_MODEL), lambda i: (i, 0)),
        out_shape=jax.ShapeDtypeStruct((TOK, D_MODEL), BF16),
        compiler_params=_params(1),
        name="norm_mod",
    )(*xs, g, sc, sh)


def _final_norm_kernel(x_ref, g_ref, o_ref):
    o_ref[...] = _rms(x_ref[...]) * g_ref[...]


def _final_norm(x, g, row0, rows):
    tr = TM
    return pl.pallas_call(
        _final_norm_kernel,
        grid=(rows // tr,),
        in_specs=[pl.BlockSpec((tr, D_MODEL), lambda i: (row0 // tr + i, 0)),
                  pl.BlockSpec((1, D_MODEL), lambda i: (0, 0))],
        out_specs=pl.BlockSpec((tr, D_MODEL), lambda i: (i, 0)),
        out_shape=jax.ShapeDtypeStruct((rows, D_MODEL), F32),
        compiler_params=_params(1),
        name="final_norm",
    )(x, g)


def _mm_residual_kernel(*refs, n_first):
    a_ref, w_ref, *x_refs, gt_lo_ref, gt_hi_ref, o_ref = refs
    x = _stream_block(x_refs, n_first)
    acc = _bdot(a_ref[...], w_ref[...])
    half = acc.shape[0] // 2
    o_ref[:half, :] = x[:half] + gt_lo_ref[...] * acc[:half]
    o_ref[half:, :] = x[half:] + gt_hi_ref[...] * acc[half:]


def _matmul_residual(a, w, l, xs, gt, tn, tm, name):
    m, k = a.shape
    n = w.shape[2]
    x_specs, n_first = _stream_specs(xs, (tm, tn), lambda j: j)
    return pl.pallas_call(
        functools.partial(_mm_residual_kernel, n_first=n_first),
        grid=(m // tm, n // tn),
        in_specs=[pl.BlockSpec((tm, k), lambda i, j: (i, 0)),
                  pl.BlockSpec((None, k, tn), lambda i, j: (l, 0, j))] + x_specs + [
                  pl.BlockSpec((None, 1, tn), lambda i, j: (_cond_of_row(i * tm), 0, j)),
                  pl.BlockSpec((None, 1, tn), lambda i, j: (_cond_of_row(i * tm + tm // 2), 0, j))],
        out_specs=pl.BlockSpec((tm, tn), lambda i, j: (i, j)),
        out_shape=jax.ShapeDtypeStruct((m, n), F32),
        compiler_params=_params(2),
        name=name,
    )(a, w, *xs, gt, gt)


def _tail_kernel(h_ref, wt_ref, o_ref):
    o_ref[...] = _bdot_t(h_ref[...], wt_ref[...]).astype(o_ref.dtype)


def _tail_proj(h, w_in_t, l):
    tn = 512
    in_w = w_in_t.shape[1]
    return pl.pallas_call(
        _tail_kernel,
        grid=(TOK // TMW, TAIL_W // tn),
        in_specs=[pl.BlockSpec((TMW, D_MODEL), lambda i, j: (i, 0)),
                  pl.BlockSpec((pl.Element(tn), pl.Element(D_MODEL)),
                               lambda i, j: (pl.multiple_of(l * in_w + IN_GLU + j * tn, SUBLANES), 0))],
        out_specs=pl.BlockSpec((TMW, tn), lambda i, j: (i, j)),
        out_shape=jax.ShapeDtypeStruct((TOK, TAIL_W), BF16),
        compiler_params=_params(2),
        name="tail_proj",
    )(h, w_in_t.reshape(DEPTH * in_w, D_MODEL))


def _mixer_a_kernel(h_ref, wh_ref, wb_ref, wc_ref, cw_ref, o_ref):
    is_ctx = pl.program_id(0) < CTX_TOK // TMW
    h = h_ref[...]
    p = _bdot_t(h, wc_ref[...]) * _bdot_t(h, wh_ref[...])
    o_ref[...] = (_bdot_t(h, wb_ref[...]) * _conv3_rows(p, cw_ref[...], is_ctx)).astype(o_ref.dtype)


def _mixer_a(h, w_in, conv_a, l):
    tn = 256
    col = lambda off: (lambda i, j: (l, off // tn + j, 0))
    wspec = lambda off: pl.BlockSpec((None, tn, D_MODEL), col(off))
    return pl.pallas_call(
        _mixer_a_kernel,
        grid=(TOK // TMW, A_W // tn),
        in_specs=[pl.BlockSpec((TMW, D_MODEL), lambda i, j: (i, 0)),
                  wspec(IN_AH), wspec(IN_AB), wspec(IN_AC),
                  pl.BlockSpec((None, A_KERNEL, tn), lambda i, j: (l, 0, j))],
        out_specs=pl.BlockSpec((TMW, tn), lambda i, j: (i, j)),
        out_shape=jax.ShapeDtypeStruct((TOK, A_W), BF16),
        compiler_params=_params(2),
        name="mixer_a",
    )(h, w_in, w_in, w_in, conv_a)


def _q_kernel(h_ref, wa_ref, g_ref, w_ref, o_ref, qn_ref):
    @pl.when(pl.program_id(1) == 0)
    def _():
        qn_ref[...] = (_rms(_bdot_t(h_ref[...], wa_ref[...])) * g_ref[...]).astype(BF16)

    o_ref[...] = _bdot(qn_ref[...], w_ref[...]).astype(o_ref.dtype)


def _q_proj(h, w_in, g_q, w_q, l):
    tn = 1024
    n = w_q.shape[2]
    return pl.pallas_call(
        _q_kernel,
        grid=(TOK // TMW, n // tn),
        in_specs=[pl.BlockSpec((TMW, D_MODEL), lambda i, j: (i, 0)),
                  pl.BlockSpec((None, Q_RANK, D_MODEL), lambda i, j: (l, IN_Q // Q_RANK, 0),
                               pipeline_mode=pl.Buffered(1)),
                  pl.BlockSpec((1, Q_RANK), lambda i, j: (0, 0)),
                  pl.BlockSpec((None, Q_RANK, tn), lambda i, j: (l, 0, j))],
        out_specs=pl.BlockSpec((TMW, tn), lambda i, j: (i, j)),
        out_shape=jax.ShapeDtypeStruct((TOK, n), BF16),
        scratch_shapes=[pltpu.VMEM((TMW, Q_RANK), BF16)],
        compiler_params=_params(2),
        name="q_proj",
    )(h, w_in, g_q, w_q)


def _kv_kernel(h_ref, wa_ref, g_ref, w_ref, ckv_ref, kpe_ref, kv_ref, cn_ref):
    @pl.when(pl.program_id(1) == 0)
    def _():
        kv_a = _bdot_t(h_ref[...], wa_ref[...])
        y = _rms(kv_a[:, :KV_RANK]) * g_ref[...]
        kpe_ref[...] = kv_a[:, KV_RANK:]
        cn_ref[...] = y.astype(BF16)

        @pl.when(pl.program_id(0) < CTX_TOK // TMW)
        def _():
            ckv_ref[...] = y

    kv_ref[...] = _bdot(cn_ref[...], w_ref[...]).astype(kv_ref.dtype)


def _kv_proj(h, w_in, g_kv, w_ukv, l):
    tn = 512
    n = w_ukv.shape[2]
    last_ctx = CTX_TOK // TMW - 1
    return pl.pallas_call(
        _kv_kernel,
        grid=(TOK // TMW, n // tn),
        in_specs=[pl.BlockSpec((TMW, D_MODEL), lambda i, j: (i, 0)),
                  pl.BlockSpec((None, KV_A_W, D_MODEL), lambda i, j: (l, IN_KV // KV_A_W, 0),
                               pipeline_mode=pl.Buffered(1)),
                  pl.BlockSpec((1, KV_RANK), lambda i, j: (0, 0)),
                  pl.BlockSpec((None, KV_RANK, tn), lambda i, j: (l, 0, j))],
        out_specs=[pl.BlockSpec((TMW, KV_RANK), lambda i, j: (jnp.minimum(i, last_ctx), 0)),
                   pl.BlockSpec((TMW, LANES), lambda i, j: (i, 0)),
                   pl.BlockSpec((TMW, tn), lambda i, j: (i, j))],
        out_shape=[jax.ShapeDtypeStruct((CTX_TOK, KV_RANK), F32),
                   jax.ShapeDtypeStruct((TOK, LANES), F32),
                   jax.ShapeDtypeStruct((TOK, n), BF16)],
        scratch_shapes=[pltpu.VMEM((TMW, KV_RANK), BF16)],
        compiler_params=_params(2),
        name="kv_proj",
    )(h, w_in, g_kv, w_ukv)


def _cache_kv_kernel(a_ref, w_ref, o_ref):
    o_ref[...] = _bdot(a_ref[...].astype(BF16), w_ref[...]).astype(o_ref.dtype)


def _cache_kv_proj(cache_ckv, w_ukv, l):
    tn = 1024
    n = w_ukv.shape[2]
    return pl.pallas_call(
        _cache_kv_kernel,
        grid=(DEC_BATCH, n // tn),
        in_specs=[pl.BlockSpec((None, None, PAST_LEN, KV_RANK), lambda b, j: (b, l, 0, 0)),
                  pl.BlockSpec((None, KV_RANK, tn), lambda b, j: (l, 0, j))],
        out_specs=pl.BlockSpec((PAST_LEN, tn), lambda b, j: (b, j)),
        out_shape=jax.ShapeDtypeStruct((DEC_BATCH * PAST_LEN, n), BF16),
        compiler_params=_params(2),
        name="cache_kv_proj",
    )(cache_ckv, w_ukv)


def _merge_kernel(a1_ref, w1_ref, a2_ref, w2_ref, a3_ref, w3_ref, g1_ref, g2_ref, g3_ref, o_ref):
    def branch(a_ref, w_ref, g_ref):
        return (1.0 + jnp.tanh(0.5 * g_ref[...].astype(F32))) * _bdot(a_ref[...], w_ref[...])

    o_ref[...] = (0.5 * (branch(a1_ref, w1_ref, g1_ref) + branch(a2_ref, w2_ref, g2_ref)
                         + branch(a3_ref, w3_ref, g3_ref))).astype(o_ref.dtype)


def _merge(ya, w_a_out, attn, w_mla_out, yc, w_c_out, tail, l):
    tn = 256
    gate = lambda k: (lambda i, j: (i, (TAIL_GATE + k * D_MODEL) // tn + j))
    act = lambda width: pl.BlockSpec((TM, width), lambda i, j: (i, 0))
    wgt = lambda width: pl.BlockSpec((None, width, tn), lambda i, j: (l, 0, j))
    return pl.pallas_call(
        _merge_kernel,
        grid=(N_ROW_BLOCKS, D_MODEL // tn),
        in_specs=[act(A_W), wgt(A_W), act(N_HEADS * V_DIM), wgt(N_HEADS * V_DIM), act(C_W), wgt(C_W),
                  pl.BlockSpec((TM, tn), gate(0)), pl.BlockSpec((TM, tn), gate(1)),
                  pl.BlockSpec((TM, tn), gate(2))],
        out_specs=pl.BlockSpec((TM, tn), lambda i, j: (i, j)),
        out_shape=jax.ShapeDtypeStruct((TOK, D_MODEL), BF16),
        compiler_params=_params(2),
        name="merge",
    )(ya, w_a_out, attn, w_mla_out, yc, w_c_out, tail, tail, tail)


FFN_TN = 512
FFN_COL_TILES = D_FF // FFN_TN


def _ffn_up_kernel(h_ref, wa_ref, wg_ref, ca_ref, cg_ref, o_ref):
    is_ctx = pl.program_id(0) < CTX_BLOCKS
    h = h_ref[...]
    ua = _bdot(h, wa_ref[...])
    ug = _bdot(h, wg_ref[...])
    a = _conv3_rows(ua, ca_ref[...], is_ctx)
    g = _conv3_rows(ug, cg_ref[...], is_ctx)
    o_ref[...] = (_silu(a) * g).astype(o_ref.dtype)


def _ffn_up(h, w_up, conv_f, l):
    tn = FFN_TN
    half = FFN_COL_TILES
    return pl.pallas_call(
        _ffn_up_kernel,
        grid=(N_ROW_BLOCKS, half),
        in_specs=[pl.BlockSpec((TM, D_MODEL), lambda i, j: (i, 0)),
                  pl.BlockSpec((None, D_MODEL, tn), lambda i, j: (l, 0, j)),
                  pl.BlockSpec((None, D_MODEL, tn), lambda i, j: (l, 0, half + j)),
                  pl.BlockSpec((None, F_KERNEL, tn), lambda i, j: (l, 0, j)),
                  pl.BlockSpec((None, F_KERNEL, tn), lambda i, j: (l, 0, half + j))],
        out_specs=pl.BlockSpec((TM, tn), lambda i, j: (i, j)),
        out_shape=jax.ShapeDtypeStruct((TOK, D_FF), BF16),
        compiler_params=_params(2),
        name="ffn_up",
    )(h, w_up, w_up, conv_f, conv_f)


def _conv_c_kernel(g1_ref, g2_ref, w_ref, lng_ref, lnb_ref, o_ref, u_ref, sh_ref, cv_ref):
    is_dec = pl.program_id(0) >= CTX_BLOCKS
    halo_w = SEQ + 2 * CONV_PAD
    n_pieces = TM // SEQ

    def gated(r0, n):
        return g1_ref[r0:r0 + n, :].astype(F32) * _sigmoid(g2_ref[r0:r0 + n, :].astype(F32))

    zeros = jnp.zeros((CONV_PAD, C_W), F32)
    u_ref[halo_w:halo_w + SUBLANES, :] = zeros[:SUBLANES]

    for p in range(n_pieces):
        r0 = p * SEQ
        top = jnp.where(is_dec, gated(r0 - CONV_PAD, CONV_PAD), 0.0) if p > 0 else zeros
        bot = jnp.where(is_dec, gated(r0 + SEQ, CONV_PAD), 0.0) if p < n_pieces - 1 else zeros
        u_ref[0:CONV_PAD, :] = top
        u_ref[CONV_PAD:CONV_PAD + SEQ, :] = gated(r0, SEQ)
        u_ref[CONV_PAD + SEQ:halo_w, :] = bot

        def shift_rows(c, carry):
            base = pl.multiple_of(c * SUBLANES, SUBLANES)
            win = u_ref[pl.ds(base, 2 * SUBLANES), :]
            for b in range(1, SUBLANES):
                sh_ref[b - 1, pl.ds(base, SUBLANES), :] = pltpu.roll(win, 2 * SUBLANES - b, 0)[:SUBLANES]
            return carry

        lax.fori_loop(0, halo_w // SUBLANES, shift_rows, 0)

        for s in range(C_W // LANES):
            cols = slice(s * LANES, (s + 1) * LANES)
            taps = [jnp.broadcast_to(w_ref[k:k + 1, cols], (SUBLANES, LANES)) for k in range(C_KERNEL)]

            def conv_rows(c, carry, cols=cols, taps=taps):
                base = pl.multiple_of(c * CONV_ROWS, CONV_ROWS)
                for r in range(0, CONV_ROWS, SUBLANES):
                    acc = None
                    for k in range(C_KERNEL):
                        off = CONV_PAD - C_KERNEL // 2 + k
                        start = pl.multiple_of(base + r + (off // SUBLANES) * SUBLANES, SUBLANES)
                        b = off % SUBLANES
                        shifted = sh_ref.at[b - 1] if b else u_ref
                        term = taps[k] * shifted[pl.ds(start, SUBLANES), cols]
                        acc = term if acc is None else acc + term
                    cv_ref[pl.ds(base + r, SUBLANES), cols] = acc
                return carry

            lax.fori_loop(0, SEQ // CONV_ROWS, conv_rows, 0)

        for base in range(0, SEQ, NORM_ROWS):
            v = cv_ref[base:base + NORM_ROWS, :]
            d = v - jnp.mean(v, axis=-1, keepdims=True)
            var = jnp.mean(d * d, axis=-1, keepdims=True)
            half_y = d * lax.rsqrt(var + EPS) * (0.5 * lng_ref[...]) + 0.5 * lnb_ref[...]
            o_ref[r0 + base:r0 + base + NORM_ROWS, :] = (half_y * (1.0 + jnp.tanh(half_y))).astype(o_ref.dtype)


def _conv_c(tail, conv_c, ln_g, ln_b, l):
    halo_w = SEQ + 2 * CONV_PAD
    return pl.pallas_call(
        _conv_c_kernel,
        grid=(N_ROW_BLOCKS,),
        in_specs=[pl.BlockSpec((TM, C_W), lambda i: (i, TAIL_G1 // C_W)),
                  pl.BlockSpec((TM, C_W), lambda i: (i, TAIL_G2 // C_W)),
                  pl.BlockSpec((None, C_KERNEL, C_W), lambda i: (l, 0, 0)),
                  pl.BlockSpec((1, C_W), lambda i: (0, 0)),
                  pl.BlockSpec((1, C_W), lambda i: (0, 0))],
        out_specs=pl.BlockSpec((TM, C_W), lambda i: (i, 0)),
        out_shape=jax.ShapeDtypeStruct((TOK, C_W), BF16),
        scratch_shapes=[pltpu.VMEM((halo_w + SUBLANES, C_W), F32),
                        pltpu.VMEM((SUBLANES - 1, halo_w, C_W), F32),
                        pltpu.VMEM((SEQ, C_W), F32)],
        compiler_params=_params(1),
        name="conv_c",
    )(tail, tail, conv_c, ln_g, ln_b)


def _rope_pairs(x, cos, sin):
    half = ROPE_DIM // 4
    lane = lax.broadcasted_iota(jnp.int32, x.shape, 1)
    swapped = jnp.where((lane & (2 * half - 1)) < half,
                        pltpu.roll(x, LANES - half, 1), pltpu.roll(x, half, 1))
    return x * cos + swapped * sin


def _pe_keys(k):
    lane = lax.broadcasted_iota(jnp.int32, k.shape, 1)
    lo = jnp.where(lane < ROPE_DIM, k, 0.0)
    return lo.astype(BF16), pltpu.roll(lo, ROPE_DIM, 1).astype(BF16)


def _attend(q_ref, rows, segments, o_ref, rope_q=None):
    c = QK_DIM ** -0.5 * math.log2(math.e)
    pe_col0 = N_HEADS * NOPE_DIM
    ones = [jnp.ones((pe[0].shape[0], V_DIM), BF16) for _, _, pe in segments]
    for hp in range(HEAD_PAIRS):
        qp = q_ref[rows, pe_col0 + hp * LANES:pe_col0 + (hp + 1) * LANES]
        if rope_q is not None:
            qp = _rope_pairs(qp.astype(F32), *rope_q).astype(BF16)
        for h in range(2):
            head = 2 * hp + h
            q = jnp.concatenate([q_ref[rows, head * NOPE_DIM:(head + 1) * NOPE_DIM], qp], axis=1)
            kbase = head * (NOPE_DIM + V_DIM)
            scores = []
            for kv_ref, krows, pe in segments:
                k = jnp.concatenate([kv_ref[krows, kbase:kbase + NOPE_DIM], pe[h]], axis=1)
                scores.append(lax.dot_general(q, k, (((1,), (1,)), ((), ())),
                                              preferred_element_type=F32))
            m = scores[0].max(axis=-1, keepdims=True)
            for s in scores[1:]:
                m = jnp.maximum(m, s.max(axis=-1, keepdims=True))
            acc = 0.0
            for s, (kv_ref, krows, _), one in zip(scores, segments, ones):
                p = jnp.exp2((s - m) * c)
                v = jnp.concatenate([kv_ref[krows, kbase + NOPE_DIM:kbase + NOPE_DIM + V_DIM], one], axis=1)
                acc = acc + jnp.dot(p.astype(BF16), v, preferred_element_type=F32)
            o_ref[rows, head * V_DIM:(head + 1) * V_DIM] = (acc[:, :V_DIM] / acc[:, V_DIM:]).astype(o_ref.dtype)


def _attn_kernel(q_ref, kvx_ref, kpex_ref, kvd_ref, kped_ref, kvc_ref, kpec_ref,
                 cq_ref, sq_ref, ck_ref, sk_ref, o_ref, lo_ref, hi_ref):
    step = pl.program_id(0)
    everything = slice(None)
    ctx_steps = CTX_TOK // TQ

    @pl.when(step < CTX_TOK // TQ)
    def _():
        for r0 in range(0, TQ, SEQ):
            rows = slice(r0, r0 + SEQ)
            _attend(q_ref, rows, [(kvx_ref, rows, _pe_keys(kpex_ref[rows, :]))], o_ref)

    @pl.when((step >= ctx_steps) & ((step - ctx_steps) % (DEC_SEQ // TQ) == 0))
    def _():
        lo, hi = _pe_keys(_rope_pairs(kped_ref[...], ck_ref[...], sk_ref[...]))
        lo_ref[...] = lo
        hi_ref[...] = hi

    @pl.when(step >= ctx_steps)
    def _():
        _attend(q_ref, everything,
                [(kvc_ref, everything, _pe_keys(kpec_ref[...])),
                 (kvd_ref, everything, (lo_ref[...], hi_ref[...]))],
                o_ref, rope_q=(cq_ref[...], sq_ref[...]))


def _attention(q, kv, kpe, kv_cache, kpe_cache, cos, sin):
    per_seq = DEC_SEQ // TQ
    ctx_steps = CTX_TOK // TQ
    n_steps = ctx_steps + DEC_BATCH * per_seq
    ctx = lambda s: jnp.minimum(s, ctx_steps - 1)
    dec = lambda s: jnp.maximum(s - ctx_steps, 0)
    kv_w = N_HEADS * (NOPE_DIM + V_DIM)
    in_specs = [
        pl.BlockSpec((TQ, q.shape[1]), lambda s: (s, 0)),
        pl.BlockSpec((TQ, kv_w), lambda s: (ctx(s), 0)),
        pl.BlockSpec((TQ, LANES), lambda s: (ctx(s), 0)),
        pl.BlockSpec((DEC_SEQ, kv_w), lambda s: (CTX_TOK // DEC_SEQ + dec(s) // per_seq, 0)),
        pl.BlockSpec((DEC_SEQ, LANES), lambda s: (CTX_TOK // DEC_SEQ + dec(s) // per_seq, 0)),
        pl.BlockSpec((PAST_LEN, kv_w), lambda s: (dec(s) // per_seq, 0)),
        pl.BlockSpec((None, PAST_LEN, LANES), lambda s: (dec(s) // per_seq, 0, 0)),
        pl.BlockSpec((TQ, LANES), lambda s: (dec(s) % per_seq, 0)),
        pl.BlockSpec((TQ, LANES), lambda s: (dec(s) % per_seq, 0)),
        pl.BlockSpec((DEC_SEQ, LANES), lambda s: (0, 0)),
        pl.BlockSpec((DEC_SEQ, LANES), lambda s: (0, 0)),
    ]
    return pl.pallas_call(
        _attn_kernel,
        grid=(n_steps,),
        in_specs=in_specs,
        out_specs=pl.BlockSpec((TQ, N_HEADS * V_DIM), lambda s: (s, 0)),
        out_shape=jax.ShapeDtypeStruct((TOK, N_HEADS * V_DIM), BF16),
        scratch_shapes=[pltpu.VMEM((DEC_SEQ, LANES), BF16)] * 2,
        compiler_params=_params(1),
        name="attention",
    )(q, kv, kpe, kv, kpe, kv_cache, kpe_cache, cos, sin, cos, sin)


def _rope_tables():
    rows = DEC_SEQ // GRID_W
    r = jnp.repeat(jnp.arange(rows), GRID_W).astype(F32)
    col = jnp.tile(jnp.arange(GRID_W), rows).astype(F32)
    half_axis = ROPE_DIM // 2
    inv = ROPE_THETA ** (-jnp.arange(0, half_axis, 2, dtype=F32) / half_axis)
    ar = r[:, None] * inv
    ac = col[:, None] * inv
    cos = jnp.concatenate([jnp.cos(ar), jnp.cos(ar), jnp.cos(ac), jnp.cos(ac)], axis=-1)
    sin = jnp.concatenate([-jnp.sin(ar), jnp.sin(ar), -jnp.sin(ac), jnp.sin(ac)], axis=-1)
    reps = LANES // ROPE_DIM
    return jnp.tile(cos, (1, reps)), jnp.tile(sin, (1, reps))


def kernel(x_prompt, x_sample, cache_ckv, cache_kpe, c, c_ctx, w_mod, b_mod, g_norm1, g_norm2, w_in, conv_a, w_a_out, g_q, w_uq, g_kv, w_ukv, w_mla_out, conv_c, ln_c_g, ln_c_b, w_c_out, w_o, w_up, conv_f, w_down, g_final):
    xs = (x_prompt.reshape(CTX_TOK, D_MODEL), x_sample.reshape(DEC_TOK, D_MODEL))
    cond = jnp.concatenate([c_ctx[None, :], c, jnp.zeros((COND_ROWS - N_COND, D_MODEL), F32)], axis=0)
    mod = _modulation(cond, w_mod, b_mod)
    cos, sin = _rope_tables()

    w_in_t = jnp.swapaxes(w_in, 1, 2)
    wq = w_uq.reshape(DEPTH, Q_RANK, N_HEADS, QK_DIM)
    w_q = jnp.concatenate([wq[..., :NOPE_DIM].reshape(DEPTH, Q_RANK, N_HEADS * NOPE_DIM),
                           wq[..., NOPE_DIM:].reshape(DEPTH, Q_RANK, N_HEADS * ROPE_DIM)],
                          axis=2).astype(BF16)
    kpe_cache = jnp.pad(cache_kpe, ((0, 0), (0, 0), (0, 0), (0, LANES - ROPE_DIM)))
    ckv_layers, kpe_layers = [], []
    for l in range(DEPTH):
        sh1, sc1, gt1, sh2, sc2, gt2 = [
            mod[l, :N_COND, k * D_MODEL:(k + 1) * D_MODEL].reshape(N_COND, 1, D_MODEL) for k in range(6)]

        h = _norm_mod(xs, g_norm1[l][None, :], sc1, sh1)
        ya = _mixer_a(h, w_in_t, conv_a, l)
        q = _q_proj(h, w_in_t, g_q[l][None, :], w_q, l)
        ckv, kpe, kv = _kv_proj(h, w_in_t, g_kv[l][None, :], w_ukv, l)
        tail = _tail_proj(h, w_in_t, l)
        kv_cache = _cache_kv_proj(cache_ckv, w_ukv, l)
        attn = _attention(q, kv, kpe, kv_cache, kpe_cache[:, l], cos, sin)
        yc = _conv_c(tail, conv_c, ln_c_g[l][None, :], ln_c_b[l][None, :], l)
        merged = _merge(ya, w_a_out, attn, w_mla_out, yc, w_c_out, tail, l)
        xs = (_matmul_residual(merged, w_o, l, xs, gt1, tn=512, tm=TMW, name="out_proj"),)

        h = _norm_mod(xs, g_norm2[l][None, :], sc2, sh2)
        act = _ffn_up(h, w_up, conv_f, l)
        xs = (_matmul_residual(act, w_down, l, xs, gt2, tn=256, tm=TM, name="ffn_down"),)

        ckv_layers.append(ckv.reshape(BATCH, SEQ, KV_RANK))
        kpe_layers.append(kpe[:CTX_TOK, :ROPE_DIM].reshape(BATCH, SEQ, ROPE_DIM))

    g = g_final[None, :]
    y_prompt = _final_norm(xs[0], g, 0, CTX_TOK).reshape(BATCH, SEQ, D_MODEL)
    y_sample = _final_norm(xs[0], g, CTX_TOK, DEC_TOK).reshape(DEC_BATCH, DEC_SEQ, D_MODEL)
    return (y_prompt, y_sample, jnp.stack(ckv_layers, axis=1), jnp.stack(kpe_layers, axis=1))
```
